```python
import jax, jax.numpy as jnp
from jax import lax
import numpy as np

D_MODEL = 4096
BATCH = 2
SEQ = 8192
DEPTH = 4

CHUNK = 64
N_META = 16
NORM_EPS = 1e-6
D_FF = 4096
HG_HEADS = 16
HG_DK = 128
HG_DV = 128
HG_WIDTH = HG_HEADS * HG_DK
HG_VWIDTH = HG_HEADS * HG_DV
ML_HEADS = 8
ML_DK = 128
ML_DV = 256
ML_QK = ML_HEADS * ML_DK
ML_V = ML_HEADS * ML_DV
CONV_W = 4
GATE_CAP = 15.0
IN_SIZES = (HG_WIDTH, HG_WIDTH, HG_VWIDTH, HG_VWIDTH,
            ML_QK, ML_QK, ML_V, ML_V, ML_HEADS, ML_HEADS,
            D_MODEL, D_MODEL)
IN_COLS = sum(IN_SIZES)

kernel_name = "hgrn2_mlstm_macaron_meta_hybrid"


def rmsnorm(x, g):
    xf = x.astype(jnp.float32)
    y = xf * lax.rsqrt(jnp.mean(xf * xf, axis=-1, keepdims=True) + NORM_EPS)
    return (y * g.astype(jnp.float32)).astype(x.dtype)


def head_rmsnorm(o, g):
    y = o * lax.rsqrt(jnp.mean(o * o, axis=-1, keepdims=True) + NORM_EPS)
    return y * g.astype(jnp.float32).reshape(o.shape[-2:])


def swiglu(x, w_gate, w_up, w_down):
    return (jax.nn.silu(x @ w_gate) * (x @ w_up)) @ w_down


def soft_cap(a):
    return GATE_CAP * jnp.tanh(a / GATE_CAP)


def to_chunks(a, pad_value):
    pad = (-a.shape[1]) % CHUNK
    widths = [(0, 0)] * a.ndim
    widths[1] = (pad, 0)
    a = jnp.pad(a, widths, constant_values=pad_value)
    nc = a.shape[1] // CHUNK
    a = a.reshape((a.shape[0], nc, CHUNK) + a.shape[2:])
    return jnp.moveaxis(jnp.moveaxis(a, 1, 0), 3, 2)


def from_chunks(a, L):
    a = jnp.moveaxis(jnp.moveaxis(a, 2, 3), 0, 1)
    a = a.reshape((a.shape[0], -1) + a.shape[3:])
    return a[:, a.shape[1] - L:]


def causal_dwconv(x, w, b):
    y = lax.conv_general_dilated(x, w[:, None, :].astype(x.dtype), window_strides=(1,),
                                 padding=[(CONV_W - 1, 0)],
                                 dimension_numbers=("NWC", "WIO", "NWC"),
                                 feature_group_count=x.shape[-1])
    return y + b.astype(x.dtype)


def hgrn2_branch(q, f_logit, v, gate, lb, norm_g):
    B, L, _ = q.shape
    f32 = jnp.float32
    q = jax.nn.silu(q.astype(f32)).reshape(B, L, HG_HEADS, HG_DK)
    z = f_logit.astype(f32)
    log_f = jnp.logaddexp(jnp.log(lb), jnp.log1p(-lb) + jax.nn.log_sigmoid(z))
    k = (-jnp.expm1(log_f)).reshape(B, L, HG_HEADS, HG_DK)
    log_f = log_f.reshape(B, L, HG_HEADS, HG_DK)
    v = v.astype(f32).reshape(B, L, HG_HEADS, HG_DV)
    qc, kc, vc, gc = to_chunks(q, 0.0), to_chunks(k, 0.0), to_chunks(v, 0.0), to_chunks(log_f, 0.0)
    causal = jnp.tril(jnp.ones((CHUNK, CHUNK), dtype=bool))[:, :, None]

    def step(S, inp):
        qb, kb, vb, gb = inp
        b = jnp.cumsum(gb, axis=2)
        o_inter = jnp.einsum("bhik,bhkv->bhiv", qb * jnp.exp(b), S)
        diff = b[:, :, :, None, :] - b[:, :, None, :, :]
        decay = jnp.exp(jnp.where(causal, diff, -jnp.inf))
        A = jnp.einsum("bhik,bhjk,bhijk->bhij", qb, kb, decay)
        o = o_inter + jnp.einsum("bhij,bhjv->bhiv", A, vb)
        b_last = b[:, :, -1:, :]
        S_new = jnp.exp(b_last[:, :, 0, :])[..., None] * S + jnp.einsum(
            "bhjk,bhjv->bhkv", kb * jnp.exp(b_last - b), vb)
        return S_new, o

    S0 = jnp.zeros((B, HG_HEADS, HG_DK, HG_DV), f32)
    _, o = lax.scan(step, S0, (qc, kc, vc, gc))
    o = from_chunks(o, L)
    o = head_rmsnorm(o, norm_g) * jax.nn.silu(gate.astype(f32)).reshape(B, L, HG_HEADS, HG_DV)
    return o.reshape(B, L, HG_VWIDTH).astype(gate.dtype)


def mlstm_branch(q, k, v, o_logit, i_logit, f_logit, conv_w, conv_b, i_bias, f_bias, norm_g):
    B, L, _ = q.shape
    f32 = jnp.float32
    qk = jax.nn.silu(causal_dwconv(jnp.concatenate([q, k], axis=-1), conv_w, conv_b)).astype(f32)
    q = qk[..., :ML_QK].reshape(B, L, ML_HEADS, ML_DK)
    k = (qk[..., ML_QK:] * (ML_DK ** -0.5)).reshape(B, L, ML_HEADS, ML_DK)
    v = v.astype(f32).reshape(B, L, ML_HEADS, ML_DV)
    i_pre = soft_cap(i_logit.astype(f32) + i_bias.astype(f32))
    log_f = jax.nn.log_sigmoid(soft_cap(f_logit.astype(f32) + f_bias.astype(f32)))
    qc, kc, vc = to_chunks(q, 0.0), to_chunks(k, 0.0), to_chunks(v, 0.0)
    ic, fc = to_chunks(i_pre, -jnp.inf), to_chunks(log_f, 0.0)
    causal = jnp.tril(jnp.ones((CHUNK, CHUNK), dtype=bool))

    def step(carry, inp):
        Cs, ns, m = carry
        qb, kb, vb, ib, fb = inp
        F = jnp.cumsum(fb, axis=-1)
        logD = jnp.where(causal, F[..., :, None] - F[..., None, :] + ib[..., None, :], -jnp.inf)
        log_prev = F + m[..., None]
        m_t = jnp.maximum(log_prev, jnp.max(logD, axis=-1))
        w_prev = jnp.exp(log_prev - m_t)
        Sqk = jnp.einsum("bhik,bhjk->bhij", qb, kb) * jnp.exp(logD - m_t[..., None])
        num = w_prev[..., None] * jnp.einsum("bhik,bhkv->bhiv", qb, Cs) + jnp.einsum("bhij,bhjv->bhiv", Sqk, vb)
        den = w_prev * jnp.einsum("bhik,bhk->bhi", qb, ns) + jnp.sum(Sqk, axis=-1)
        h = num / jnp.maximum(jnp.abs(den), jnp.exp(-m_t))[..., None]
        m_new = m_t[..., -1]
        w_old = jnp.exp(F[..., -1] + m - m_new)
        w_in = jnp.exp(F[..., -1:] - F + ib - m_new[..., None])
        Cs_new = w_old[..., None, None] * Cs + jnp.einsum("bhj,bhjk,bhjv->bhkv", w_in, kb, vb)
        ns_new = w_old[..., None] * ns + jnp.einsum("bhj,bhjk->bhk", w_in, kb)
        return (Cs_new, ns_new, m_new), h

    init = (jnp.zeros((B, ML_HEADS, ML_DK, ML_DV), f32),
            jnp.zeros((B, ML_HEADS, ML_DK), f32),
            jnp.zeros((B, ML_HEADS), f32))
    _, h = lax.scan(step, init, (qc, kc, vc, ic, fc))
    h = from_chunks(h, L)
    h = head_rmsnorm(h, norm_g) * jax.nn.sigmoid(o_logit.astype(f32)).reshape(B, L, ML_HEADS, ML_DV)
    return h.reshape(B, L, ML_V).astype(o_logit.dtype)


def mixer_block(u, w_in, lb, conv_w, conv_b, i_bias, f_bias, hg_norm, ml_norm, w_a, w_b, w_o):
    z = u @ w_in
    offs = np.cumsum(IN_SIZES)[:-1].tolist()
    hq, hf, hi, hg, mq, mk, mv, mo, mi, mf, ga, gb = jnp.split(z, offs, axis=-1)
    ya = hgrn2_branch(hq, hf, hi, hg, lb, hg_norm)
    yb = mlstm_branch(mq, mk, mv, mo, mi, mf, conv_w, conv_b, i_bias, f_bias, ml_norm)
    y = jax.nn.sigmoid(ga) * (ya @ w_a) + jax.nn.sigmoid(gb) * (yb @ w_b)
    return y @ w_o


def setup_inputs(seed: int = 0) -> dict:
    key = jax.random.key(seed)
    ks = jax.random.split(key, 24)
    f32 = jnp.float32

    def dense(k, shape, fan_in):
        return jax.random.normal(k, shape, f32) * (fan_in ** -0.5)

    def gain(k, shape):
        return 1.0 + 0.01 * jax.random.normal(k, shape, f32)

    return {
        "x": jax.random.normal(ks[0], (BATCH, SEQ, D_MODEL), f32),
        "meta_tokens": jax.random.normal(ks[1], (N_META, D_MODEL), f32),
        "hgrn_lb_logits": 0.5 * jax.random.normal(ks[2], (DEPTH, HG_WIDTH), f32),
        "norm_ffn1": gain(ks[3], (DEPTH, D_MODEL)),
        "ffn1_w_gate": dense(ks[4], (DEPTH, D_MODEL, D_FF), D_MODEL),
        "ffn1_w_up": dense(ks[5], (DEPTH, D_MODEL, D_FF), D_MODEL),
        "ffn1_w_down": dense(ks[6], (DEPTH, D_FF, D_MODEL), D_FF),
        "norm_mix": gain(ks[7], (DEPTH, D_MODEL)),
        "w_in": dense(ks[8], (DEPTH, D_MODEL, IN_COLS), D_MODEL),
        "mlstm_conv_w": dense(ks[9], (DEPTH, CONV_W, 2 * ML_QK), CONV_W),
        "mlstm_conv_b": 0.01 * jax.random.normal(ks[10], (DEPTH, 2 * ML_QK), f32),
        "mlstm_igate_b": 0.1 * jax.random.normal(ks[11], (DEPTH, ML_HEADS), f32),
        "mlstm_fgate_b": jnp.linspace(3.0, 6.0, ML_HEADS, dtype=f32)[None, :]
                         + 0.1 * jax.random.normal(ks[12], (DEPTH, ML_HEADS), f32),
        "hgrn_out_norm": gain(ks[13], (DEPTH, HG_VWIDTH)),
        "mlstm_out_norm": gain(ks[14], (DEPTH, ML_V)),
        "w_branch_a": dense(ks[15], (DEPTH, HG_VWIDTH, D_MODEL), HG_VWIDTH),
        "w_branch_b": dense(ks[16], (DEPTH, ML_V, D_MODEL), ML_V),
        "w_out": dense(ks[17], (DEPTH, D_MODEL, D_MODEL), D_MODEL),
        "norm_ffn2": gain(ks[18], (DEPTH, D_MODEL)),
        "ffn2_w_gate": dense(ks[19], (DEPTH, D_MODEL, D_FF), D_MODEL),
        "ffn2_w_up": dense(ks[20], (DEPTH, D_MODEL, D_FF), D_MODEL),
        "ffn2_w_down": dense(ks[21], (DEPTH, D_FF, D_MODEL), D_FF),
        "final_norm": gain(ks[22], (D_MODEL,)),
    }


def reference(x, meta_tokens, hgrn_lb_logits, norm_ffn1, ffn1_w_gate, ffn1_w_up, ffn1_w_down,
              norm_mix, w_in, mlstm_conv_w, mlstm_conv_b, mlstm_igate_b, mlstm_fgate_b,
              hgrn_out_norm, mlstm_out_norm, w_branch_a, w_branch_b, w_out,
              norm_ffn2, ffn2_w_gate, ffn2_w_up, ffn2_w_down, final_norm):
    B = x.shape[0]
    meta = jnp.broadcast_to(meta_tokens[None].astype(x.dtype), (B, N_META, D_MODEL))
    h = jnp.concatenate([meta, x], axis=1)
    lb_all = jnp.cumsum(jax.nn.softmax(hgrn_lb_logits.astype(jnp.float32), axis=0), axis=0)
    lb_all = lb_all - lb_all[0:1]
    for l in range(DEPTH):
        h = h + 0.5 * swiglu(rmsnorm(h, norm_ffn1[l]), ffn1_w_gate[l], ffn1_w_up[l], ffn1_w_down[l])
        h = h + mixer_block(rmsnorm(h, norm_mix[l]), w_in[l], lb_all[l], mlstm_conv_w[l], mlstm_conv_b[l],
                            mlstm_igate_b[l], mlstm_fgate_b[l], hgrn_out_norm[l], mlstm_out_norm[l],
                            w_branch_a[l], w_branch_b[l], w_out[l])
        h = h + 0.5 * swiglu(rmsnorm(h, norm_ffn2[l]), ffn2_w_gate[l], ffn2_w_up[l], ffn2_w_down[l])
    h = rmsnorm(h, final_norm)
    return h[:, N_META:]
```

```python
import functools

import jax
import jax.numpy as jnp
from jax import lax
from jax.experimental import pallas as pl
from jax.experimental.pallas import tpu as pltpu

F32 = jnp.float32
BF16 = jnp.bfloat16

CHUNK = 64
N_META = 16
NORM_EPS = 1e-6
HEAD_DK = 128
HG_DV = 128
ML_DV = 256
CONV_W = 4
GATE_CAP = 15.0
SUB = 16
LANES = 128
VMEM_LIMIT = 56 * 1024 * 1024

_HI = lax.Precision.HIGHEST
_NT = (((1,), (1,)), ((), ()))
_TN = (((0,), (0,)), ((), ()))


def _params(sem):
    return pltpu.CompilerParams(dimension_semantics=sem, vmem_limit_bytes=VMEM_LIMIT)


def _row_tile(n_rows, target):
    best = None
    for t in range(16, min(n_rows, target) + 1, 16):
        if n_rows % t == 0:
            best = t
    assert best is not None, n_rows
    return best


def _col_tile(n_cols, target):
    best = None
    for t in range(LANES, min(n_cols, target) + 1, LANES):
        if n_cols % t == 0:
            best = t
    assert best is not None, n_cols
    return best


def _log_sigmoid(z):
    return jnp.minimum(z, 0.0) - jnp.log1p(jnp.exp(-jnp.abs(z)))


def _silu(z):
    return z * jax.nn.sigmoid(z)


def _rmsnorm_kernel(h_ref, g_ref, u_ref):
    x = h_ref[...]
    y = x * lax.rsqrt(jnp.mean(x * x, axis=-1, keepdims=True) + NORM_EPS)
    u_ref[...] = (y * g_ref[...]).astype(u_ref.dtype)


def _rmsnorm_gates_kernel(h_ref, g_ref, wg_ref, u_ref, zg_ref):
    x = h_ref[...]
    y = x * lax.rsqrt(jnp.mean(x * x, axis=-1, keepdims=True) + NORM_EPS)
    u = (y * g_ref[...]).astype(u_ref.dtype)
    u_ref[...] = u
    zg_ref[...] = jnp.dot(u, wg_ref[...], preferred_element_type=F32)


def _rmsnorm(h, gain, out_dtype=BF16, w_gates=None):
    T, D = h.shape
    tm = _row_tile(T, 384)
    gain = gain.reshape(1, D).astype(F32)
    if w_gates is None:
        return pl.pallas_call(
            _rmsnorm_kernel,
            grid=(T // tm,),
            in_specs=[pl.BlockSpec((tm, D), lambda i: (i, 0)),
                      pl.BlockSpec((1, D), lambda i: (0, 0))],
            out_specs=pl.BlockSpec((tm, D), lambda i: (i, 0)),
            out_shape=jax.ShapeDtypeStruct((T, D), out_dtype),
            compiler_params=_params(("parallel",)),
            name="rmsnorm",
        )(h, gain)
    return pl.pallas_call(
        _rmsnorm_gates_kernel,
        grid=(T // tm,),
        in_specs=[pl.BlockSpec((tm, D), lambda i: (i, 0)),
                  pl.BlockSpec((1, D), lambda i: (0, 0)),
                  pl.BlockSpec((D, LANES), lambda i: (0, 0))],
        out_specs=[pl.BlockSpec((tm, D), lambda i: (i, 0)),
                   pl.BlockSpec((tm, LANES), lambda i: (i, 0))],
        out_shape=[jax.ShapeDtypeStruct((T, D), out_dtype),
                   jax.ShapeDtypeStruct((T, LANES), F32)],
        compiler_params=_params(("parallel",)),
        name="rmsnorm_gates",
    )(h, gain, w_gates)


def _final_norm_kernel(h_ref, g_ref, o_ref):
    x = h_ref[...]
    y = x * lax.rsqrt(jnp.mean(x * x, axis=-1, keepdims=True) + NORM_EPS)
    o_ref[...] = y * g_ref[...]


def _final_norm(h, gain, batch, l_pad, seq):
    T, D = h.shape
    lead = l_pad - seq
    assert lead % CHUNK == 0 and seq % CHUNK == 0
    nlead, nseq, ntot = lead // CHUNK, seq // CHUNK, l_pad // CHUNK
    return pl.pallas_call(
        _final_norm_kernel,
        grid=(batch, nseq),
        in_specs=[pl.BlockSpec((CHUNK, D), lambda b, i: (b * ntot + nlead + i, 0)),
                  pl.BlockSpec((1, D), lambda b, i: (0, 0))],
        out_specs=pl.BlockSpec((None, CHUNK, D), lambda b, i: (b, i, 0)),
        out_shape=jax.ShapeDtypeStruct((batch, seq, D), F32),
        compiler_params=_params(("parallel", "parallel")),
        name="final_norm",
    )(h, gain.reshape(1, D).astype(F32))


def _ffn_up_kernel(u_ref, wg_ref, wu_ref, a_ref):
    u = u_ref[...]
    g = jnp.dot(u, wg_ref[...], preferred_element_type=F32)
    up = jnp.dot(u, wu_ref[...], preferred_element_type=F32)
    a_ref[...] = (_silu(g) * up).astype(a_ref.dtype)


def _ffn_up(u, w_gate, w_up):
    T, D = u.shape
    F = w_gate.shape[1]
    tm, tn = _row_tile(T, 1376), _col_tile(F, 512)
    return pl.pallas_call(
        _ffn_up_kernel,
        grid=(T // tm, F // tn),
        in_specs=[pl.BlockSpec((tm, D), lambda i, j: (i, 0)),
                  pl.BlockSpec((D, tn), lambda i, j: (0, j)),
                  pl.BlockSpec((D, tn), lambda i, j: (0, j))],
        out_specs=pl.BlockSpec((tm, tn), lambda i, j: (i, j)),
        out_shape=jax.ShapeDtypeStruct((T, F), BF16),
        compiler_params=_params(("parallel", "arbitrary")),
        name="ffn_up",
    )(u, w_gate, w_up)


def _resid_mm_kernel(a_ref, w_ref, h_ref, o_ref, *, scale):
    o_ref[...] = h_ref[...] + scale * jnp.dot(a_ref[...], w_ref[...], preferred_element_type=F32)


def _resid_mm(a, w, h, scale):
    T, K = a.shape
    N = w.shape[1]
    tm, tn = _row_tile(T, 1376), _col_tile(N, 512)
    return pl.pallas_call(
        functools.partial(_resid_mm_kernel, scale=scale),
        grid=(T // tm, N // tn),
        in_specs=[pl.BlockSpec((tm, K), lambda i, j: (i, 0)),
                  pl.BlockSpec((K, tn), lambda i, j: (0, j)),
                  pl.BlockSpec((tm, tn), lambda i, j: (i, j))],
        out_specs=pl.BlockSpec((tm, tn), lambda i, j: (i, j)),
        out_shape=jax.ShapeDtypeStruct((T, N), F32),
        input_output_aliases={2: 0},
        compiler_params=_params(("parallel", "arbitrary")),
        name="resid_mm",
    )(a, w, h)


def _mm_kernel(u_ref, w_ref, o_ref):
    o_ref[...] = jnp.dot(u_ref[...], w_ref[...], preferred_element_type=F32).astype(o_ref.dtype)


def _in_proj(u, w):
    T, K = u.shape
    N = w.shape[1]
    tm, tn = _row_tile(T, 1376), _col_tile(N, 512)
    return pl.pallas_call(
        _mm_kernel,
        grid=(T // tm, N // tn),
        in_specs=[pl.BlockSpec((tm, K), lambda i, j: (i, 0)),
                  pl.BlockSpec((K, tn), lambda i, j: (0, j))],
        out_specs=pl.BlockSpec((tm, tn), lambda i, j: (i, j)),
        out_shape=jax.ShapeDtypeStruct((T, N), BF16),
        compiler_params=_params(("parallel", "arbitrary")),
        name="in_proj",
    )(u, w)


def _merge_kernel(ya_ref, yb_ref, wa_ref, wb_ref, ga_ref, gb_ref, y_ref):
    pa = jnp.dot(ya_ref[...], wa_ref[...], preferred_element_type=F32)
    pb = jnp.dot(yb_ref[...], wb_ref[...], preferred_element_type=F32)
    y = jax.nn.sigmoid(ga_ref[...].astype(F32)) * pa + jax.nn.sigmoid(gb_ref[...].astype(F32)) * pb
    y_ref[...] = y.astype(y_ref.dtype)


def _merge(ya, yb, w_a, w_b, z, ga_col, gb_col):
    T, Ka = ya.shape
    Kb = yb.shape[1]
    N = w_a.shape[1]
    tm = _row_tile(T, 1376)
    tn = max(t for t in range(LANES, 512 + 1, LANES)
             if N % t == 0 and ga_col % t == 0 and gb_col % t == 0)
    ga_blk, gb_blk = ga_col // tn, gb_col // tn
    return pl.pallas_call(
        _merge_kernel,
        grid=(T // tm, N // tn),
        in_specs=[pl.BlockSpec((tm, Ka), lambda i, j: (i, 0)),
                  pl.BlockSpec((tm, Kb), lambda i, j: (i, 0)),
                  pl.BlockSpec((Ka, tn), lambda i, j: (0, j)),
                  pl.BlockSpec((Kb, tn), lambda i, j: (0, j)),
                  pl.BlockSpec((tm, tn), lambda i, j: (i, ga_blk + j)),
                  pl.BlockSpec((tm, tn), lambda i, j: (i, gb_blk + j))],
        out_specs=pl.BlockSpec((tm, tn), lambda i, j: (i, j)),
        out_shape=jax.ShapeDtypeStruct((T, N), BF16),
        compiler_params=_params(("parallel", "arbitrary")),
        name="merge",
    )(ya, yb, w_a, w_b, z, z)


def _hgrn_kernel(lbl_ref, q_ref, f_ref, v_ref, g_ref, ng_ref, o_ref, st_ref, *, layer, n_chunks, pad):
    lbl = lbl_ref[...]
    e = jnp.exp(lbl - jnp.max(lbl, axis=0, keepdims=True))
    p = e / jnp.sum(e, axis=0, keepdims=True)
    lb = jnp.zeros((1, HEAD_DK), F32)
    for r in range(1, layer + 1):
        lb = lb + p[r:r + 1]
    log_lb = jnp.log(lb)
    log1m_lb = jnp.log1p(-lb)
    ng = ng_ref[...]

    row = lax.broadcasted_iota(jnp.int32, (CHUNK, 1), 0)
    tri_r = lax.broadcasted_iota(jnp.int32, (CHUNK, CHUNK), 0)
    tri_c = lax.broadcasted_iota(jnp.int32, (CHUNK, CHUNK), 1)
    tril = (tri_r >= tri_c).astype(F32)
    sub_row = lax.broadcasted_iota(jnp.int32, (SUB, 1), 0)
    sub_lane = lax.broadcasted_iota(jnp.int32, (SUB, CHUNK), 1)

    st_ref[...] = jnp.zeros_like(st_ref)

    def chunk(c, carry):
        r0 = pl.multiple_of(c * CHUNK, CHUNK)
        valid = (r0 + row) >= pad
        qz = q_ref[pl.ds(r0, CHUNK), :].astype(F32)
        fz = f_ref[pl.ds(r0, CHUNK), :].astype(F32)
        v = jnp.where(valid, v_ref[pl.ds(r0, CHUNK), :].astype(F32), 0.0)
        gz = g_ref[pl.ds(r0, CHUNK), :].astype(F32)

        q = jnp.where(valid, _silu(qz), 0.0)
        ls = _log_sigmoid(fz)
        a_ = log_lb
        b_ = log1m_lb + ls
        log_f = jnp.maximum(a_, b_) + jnp.log1p(jnp.exp(-jnp.abs(a_ - b_)))
        k = jnp.where(valid, jnp.exp(log1m_lb + ls - fz), 0.0)
        log_f = jnp.where(valid, log_f, 0.0)

        b = jnp.dot(tril, log_f, precision=_HI, preferred_element_type=F32)
        b_last = b[CHUNK - 1:CHUNK]
        v16 = v.astype(BF16)

        st = st_ref[...]
        o = lax.dot_general((q * jnp.exp(b)).astype(BF16), st.astype(BF16), _NT,
                            preferred_element_type=F32)

        a_rows = []
        for blk in range(CHUNK // SUB):
            lo = blk * SUB
            q_i, k_i, b_i = q[lo:lo + SUB], k[lo:lo + SUB], b[lo:lo + SUB]
            if blk == 0:
                a_i = jnp.zeros((SUB, CHUNK), F32)
            else:
                beta = b[lo - 1:lo]
                qt = q_i * jnp.exp(b_i - beta)
                kt = jnp.where(row < lo, k * jnp.exp(jnp.minimum(beta - b, 0.0)), 0.0)
                a_i = lax.dot_general(qt.astype(BF16), kt.astype(BF16), _NT,
                                      preferred_element_type=F32)
            for jj in range(SUB):
                d = jnp.exp(jnp.minimum(b_i - b_i[jj:jj + 1], 0.0))
                s = jnp.sum(q_i * k_i[jj:jj + 1] * d, axis=-1, keepdims=True)
                s = jnp.where(sub_row >= jj, s, 0.0)
                a_i = jnp.where(sub_lane == lo + jj, s, a_i)
            a_rows.append(a_i)
        a_mat = jnp.concatenate(a_rows, axis=0)
        o = o + jnp.dot(a_mat.astype(BF16), v16, preferred_element_type=F32)

        khat = (k * jnp.exp(b_last - b)).astype(BF16)
        st_ref[...] = st * jnp.exp(b_last) + lax.dot_general(v16, khat, _TN,
                                                            preferred_element_type=F32)

        y = o * lax.rsqrt(jnp.mean(o * o, axis=-1, keepdims=True) + NORM_EPS)
        o_ref[pl.ds(r0, CHUNK), :] = (y * ng * _silu(gz)).astype(o_ref.dtype)
        return carry

    lax.fori_loop(0, n_chunks, chunk, 0)


def _hgrn(z, lb_logits, norm_g, layer, batch, l_pad, heads, pad):
    T = z.shape[0]
    depth = lb_logits.shape[0]
    kern = functools.partial(_hgrn_kernel, layer=layer, n_chunks=l_pad // CHUNK, pad=pad)
    seq_blk = lambda off: pl.BlockSpec((l_pad, HEAD_DK), lambda b, h: (b, off + h))
    return pl.pallas_call(
        kern,
        grid=(batch, heads),
        in_specs=[pl.BlockSpec((depth, HEAD_DK), lambda b, h: (0, h)),
                  seq_blk(0), seq_blk(heads), seq_blk(2 * heads), seq_blk(3 * heads),
                  pl.BlockSpec((1, HG_DV), lambda b, h: (0, h))],
        out_specs=pl.BlockSpec((l_pad, HG_DV), lambda b, h: (b, h)),
        out_shape=jax.ShapeDtypeStruct((T, heads * HG_DV), BF16),
        scratch_shapes=[pltpu.VMEM((HG_DV, HEAD_DK), F32)],
        compiler_params=_params(("parallel", "parallel")),
        name="hgrn2",
    )(lb_logits.astype(F32), z, z, z, z, norm_g.reshape(1, -1).astype(F32))


def _mlstm_kernel(zg_ref, gbias_ref, q_ref, k_ref, v_ref, og_ref, cwq_ref, cwk_ref, cbq_ref, cbk_ref,
                  ng_ref, o_ref, cs_ref, ns_ref, m_ref, *, n_chunks, pad, heads):
    h_idx = pl.program_id(1)
    row = lax.broadcasted_iota(jnp.int32, (CHUNK, 1), 0)
    lane = lax.broadcasted_iota(jnp.int32, (CHUNK, LANES), 1)
    tri_r = lax.broadcasted_iota(jnp.int32, (CHUNK, CHUNK), 0)
    tri_c = lax.broadcasted_iota(jnp.int32, (CHUNK, CHUNK), 1)
    causal = tri_r >= tri_c
    tril = causal.astype(F32)
    eye = tri_r == tri_c
    gbias = gbias_ref[...]
    cwq, cwk = cwq_ref[...], cwk_ref[...]
    cbq, cbk = cbq_ref[...], cbk_ref[...]
    ng = ng_ref[...]
    k_scale = HEAD_DK ** -0.5

    cs_ref[...] = jnp.zeros_like(cs_ref)
    ns_ref[...] = jnp.zeros_like(ns_ref)
    m_ref[...] = jnp.zeros_like(m_ref)

    def conv_silu(x_ref, r0, c, valid, valid_prev, w, bias):
        cur = jnp.where(valid, x_ref[pl.ds(r0, CHUNK), :].astype(F32), 0.0)
        rp = pl.multiple_of(jnp.maximum(r0 - CHUNK, 0), CHUNK)
        prev = jnp.where(jnp.logical_and(valid_prev, c > 0), x_ref[pl.ds(rp, CHUNK), :].astype(F32), 0.0)
        y = cur * w[CONV_W - 1:CONV_W] + bias
        for s in range(1, CONV_W):
            shifted = jnp.where(row < s, pltpu.roll(prev, s, 0), pltpu.roll(cur, s, 0))
            y = y + shifted * w[CONV_W - 1 - s:CONV_W - s]
        return _silu(y)

    def chunk(c, carry):
        r0 = pl.multiple_of(c * CHUNK, CHUNK)
        valid = (r0 + row) >= pad
        valid_prev = (r0 - CHUNK + row) >= pad
        valid_col = (r0 + tri_c) >= pad

        q = jnp.where(valid, conv_silu(q_ref, r0, c, valid, valid_prev, cwq, cbq), 0.0)
        k = jnp.where(valid, conv_silu(k_ref, r0, c, valid, valid_prev, cwk, cbk) * k_scale, 0.0)
        v16 = jnp.where(valid, v_ref[pl.ds(r0, CHUNK), :], jnp.zeros((), BF16))
        q16 = q.astype(BF16)

        zg = zg_ref[pl.ds(r0, CHUNK), :] + gbias
        i_raw = jnp.sum(jnp.where(lane == h_idx, zg, 0.0), axis=-1, keepdims=True)
        f_raw = jnp.sum(jnp.where(lane == heads + h_idx, zg, 0.0), axis=-1, keepdims=True)
        i_fin = GATE_CAP * jnp.tanh(i_raw / GATE_CAP)
        i_pre = jnp.where(valid, i_fin, -jnp.inf)
        log_f = jnp.where(valid, _log_sigmoid(GATE_CAP * jnp.tanh(f_raw / GATE_CAP)), 0.0)

        fcum_b = jnp.dot(tril, jnp.broadcast_to(log_f, (CHUNK, LANES)), precision=_HI,
                         preferred_element_type=F32)
        fcum = fcum_b[:, :1]
        g = fcum - i_fin
        g_row = jnp.sum(jnp.where(eye, jnp.broadcast_to(g, (CHUNK, CHUNK)), 0.0),
                        axis=0, keepdims=True)
        log_d = jnp.where(jnp.logical_and(causal, valid_col), fcum - g_row, -jnp.inf)

        m_prev = m_ref[...]
        log_prev = fcum + m_prev
        m_t = jnp.maximum(log_prev, jnp.max(log_d, axis=-1, keepdims=True))
        w_prev = jnp.exp(log_prev - m_t)
        sqk = lax.dot_general(q16, k.astype(BF16), _NT, preferred_element_type=F32) * jnp.exp(log_d - m_t)

        cs = cs_ref[...]
        ns = ns_ref[...]
        num = w_prev * jnp.dot(q16, cs.astype(BF16), preferred_element_type=F32) \
            + jnp.dot(sqk.astype(BF16), v16, preferred_element_type=F32)
        den = w_prev * jnp.sum(q * ns, axis=-1, keepdims=True) + jnp.sum(sqk, axis=-1, keepdims=True)
        hval = num / jnp.maximum(jnp.abs(den), jnp.exp(-m_t))

        m_new = m_t[CHUNK - 1:CHUNK]
        f_last = fcum[CHUNK - 1:CHUNK]
        w_old = jnp.exp(f_last + m_prev - m_new)
        w_in = jnp.exp(f_last - fcum + i_pre - m_new)
        kw = k * w_in
        cs_ref[...] = w_old * cs + lax.dot_general(kw.astype(BF16), v16, _TN, preferred_element_type=F32)
        ns_ref[...] = w_old * ns + jnp.sum(kw, axis=0, keepdims=True)
        m_ref[...] = m_new

        y = hval * lax.rsqrt(jnp.mean(hval * hval, axis=-1, keepdims=True) + NORM_EPS)
        og = og_ref[pl.ds(r0, CHUNK), :].astype(F32)
        o_ref[pl.ds(r0, CHUNK), :] = (y * ng * jax.nn.sigmoid(og)).astype(o_ref.dtype)
        return carry

    lax.fori_loop(0, n_chunks, chunk, 0)


def _mlstm(z, zg, gate_bias, conv_w, conv_b, norm_g, batch, l_pad, heads, q_col, pad):
    T = z.shape[0]
    qk_w = heads * HEAD_DK
    qb = q_col // HEAD_DK
    kb = qb + heads
    vb = (q_col + 2 * qk_w) // ML_DV
    ob = vb + heads
    kern = functools.partial(_mlstm_kernel, n_chunks=l_pad // CHUNK, pad=pad, heads=heads)
    return pl.pallas_call(
        kern,
        grid=(batch, heads),
        in_specs=[pl.BlockSpec((l_pad, LANES), lambda b, h: (b, 0)),
                  pl.BlockSpec((1, LANES), lambda b, h: (0, 0)),
                  pl.BlockSpec((l_pad, HEAD_DK), lambda b, h: (b, qb + h)),
                  pl.BlockSpec((l_pad, HEAD_DK), lambda b, h: (b, kb + h)),
                  pl.BlockSpec((l_pad, ML_DV), lambda b, h: (b, vb + h)),
                  pl.BlockSpec((l_pad, ML_DV), lambda b, h: (b, ob + h)),
                  pl.BlockSpec((CONV_W, HEAD_DK), lambda b, h: (0, h)),
                  pl.BlockSpec((CONV_W, HEAD_DK), lambda b, h: (0, heads + h)),
                  pl.BlockSpec((1, HEAD_DK), lambda b, h: (0, h)),
                  pl.BlockSpec((1, HEAD_DK), lambda b, h: (0, heads + h)),
                  pl.BlockSpec((1, ML_DV), lambda b, h: (0, h))],
        out_specs=pl.BlockSpec((l_pad, ML_DV), lambda b, h: (b, h)),
        out_shape=jax.ShapeDtypeStruct((T, heads * ML_DV), BF16),
        scratch_shapes=[pltpu.VMEM((HEAD_DK, ML_DV), F32),
                        pltpu.VMEM((1, HEAD_DK), F32),
                        pltpu.VMEM((1, 1), F32)],
        compiler_params=_params(("parallel", "parallel")),
        name="mlstm",
    )(zg, gate_bias, z, z, z, z, conv_w.astype(F32), conv_w.astype(F32),
      conv_b.reshape(1, -1).astype(F32), conv_b.reshape(1, -1).astype(F32),
      norm_g.reshape(1, -1).astype(F32))


def kernel(x, meta_tokens, hgrn_lb_logits, norm_ffn1, ffn1_w_gate, ffn1_w_up, ffn1_w_down, norm_mix, w_in, mlstm_conv_w, mlstm_conv_b, mlstm_igate_b, mlstm_fgate_b, hgrn_out_norm, mlstm_out_norm, w_branch_a, w_branch_b, w_out, norm_ffn2, ffn2_w_gate, ffn2_w_up, ffn2_w_down, final_norm):
    batch, seq, d_model = x.shape
    depth = w_in.shape[0]
    n_meta = meta_tokens.shape[0]
    hg_heads = w_branch_a.shape[1] // HG_DV
    ml_heads = mlstm_igate_b.shape[1]
    hg_w = hg_heads * HEAD_DK
    ml_qk = ml_heads * HEAD_DK
    ml_v = ml_heads * ML_DV
    assert 2 * ml_heads <= LANES

    pad = (-(n_meta + seq)) % CHUNK
    l_pad = pad + n_meta + seq
    meta = jnp.broadcast_to(meta_tokens[None].astype(x.dtype), (batch, n_meta, d_model))
    h = jnp.concatenate([jnp.zeros((batch, pad, d_model), x.dtype), meta, x], axis=1)
    h = h.reshape(batch * l_pad, d_model)

    gates_col = 4 * hg_w + 2 * ml_qk + 2 * ml_v
    n_gate = 2 * ml_heads
    mlstm_q_col = 4 * hg_w
    ga_col = gates_col
    gb_col = gates_col + d_model

    for l in range(depth):
        w_main = jnp.concatenate([w_in[l, :, :gates_col], w_in[l, :, gates_col + n_gate:]], axis=1).astype(BF16)
        w_gates = jnp.pad(w_in[l, :, gates_col:gates_col + n_gate], ((0, 0), (0, LANES - n_gate))).astype(BF16)
        gate_bias = jnp.pad(jnp.concatenate([mlstm_igate_b[l], mlstm_fgate_b[l]]).astype(F32),
                            (0, LANES - n_gate)).reshape(1, LANES)

        u = _rmsnorm(h, norm_ffn1[l])
        a = _ffn_up(u, ffn1_w_gate[l].astype(BF16), ffn1_w_up[l].astype(BF16))
        h = _resid_mm(a, ffn1_w_down[l].astype(BF16), h, 0.5)

        u, zg = _rmsnorm(h, norm_mix[l], w_gates=w_gates)
        z = _in_proj(u, w_main)
        ya = _hgrn(z, hgrn_lb_logits, hgrn_out_norm[l], l, batch, l_pad, hg_heads, pad)
        yb = _mlstm(z, zg, gate_bias, mlstm_conv_w[l], mlstm_conv_b[l], mlstm_out_norm[l],
                    batch, l_pad, ml_heads, mlstm_q_col, pad)
        y = _merge(ya, yb, w_branch_a[l].astype(BF16), w_branch_b[l].astype(BF16), z, ga_col, gb_col)
        h = _resid_mm(y, w_out[l].astype(BF16), h, 1.0)

        u = _rmsnorm(h, norm_ffn2[l])
        a = _ffn_up(u, ffn2_w_gate[l].astype(BF16), ffn2_w_up[l].astype(BF16))
        h = _resid_mm(a, ffn2_w_down[l].astype(BF16), h, 0.5)

    return _final_norm(h, final_norm, batch, l_pad, seq)
```

```python
import functools

import jax
import jax.numpy as jnp
from jax import lax
from jax.experimental import pallas as pl
from jax.experimental.pallas import tpu as pltpu

F32 = jnp.float32
BF16 = jnp.bfloat16

CHUNK = 64
NORM_EPS = 1e-6
HEAD_DK = 128
HG_DV = 128
ML_DV = 256
CONV_W = 4
GATE_CAP = 15.0
SUB = 16
SUBLANES = 8
LANES = 128
VMEM_LIMIT = 56 * 1024 * 1024
HG_GROUP = 4
ML_GROUP = 4
HG_UNROLL = 3
ML_UNROLL = 2
MAX_BLOCK_CHUNKS = 48
LOG2E = 1.4426950408889634

_HI = lax.Precision.HIGHEST
_NT = (((1,), (1,)), ((), ()))
_TN = (((0,), (0,)), ((), ()))


def _params(sem):
    return pltpu.CompilerParams(dimension_semantics=sem, vmem_limit_bytes=VMEM_LIMIT)


def _row_tile(n_rows, target):
    return max(t for t in range(16, min(n_rows, target) + 1, 16) if n_rows % t == 0)


def _col_tile(n_cols, target):
    return max(t for t in range(LANES, min(n_cols, target) + 1, LANES) if n_cols % t == 0)


def _block_chunks(n_chunks):
    return max(d for d in range(1, min(n_chunks, MAX_BLOCK_CHUNKS) + 1) if n_chunks % d == 0)


def _group(heads, target):
    return max(g for g in range(1, target + 1) if heads % g == 0)


def _for_chunks(n, body, unroll):
    trips = n // unroll

    def trip(i, carry):
        for u in range(unroll):
            body(i * unroll + u)
        return carry

    if trips > 0:
        lax.fori_loop(0, trips, trip, 0)
    for c in range(trips * unroll, n):
        body(c)


def _log_sigmoid(z):
    return jnp.minimum(z, 0.0) - jnp.log1p(jnp.exp(-jnp.abs(z)))


def _silu(z):
    return z * jax.nn.sigmoid(z)


def _rmsnorm_kernel(h_ref, g_ref, u_ref):
    x = h_ref[...]
    y = x * lax.rsqrt(jnp.mean(x * x, axis=-1, keepdims=True) + NORM_EPS)
    u_ref[...] = (y * g_ref[...]).astype(u_ref.dtype)


def _rmsnorm_gates_kernel(h_ref, g_ref, wg_ref, u_ref, zg_ref):
    x = h_ref[...]
    y = x * lax.rsqrt(jnp.mean(x * x, axis=-1, keepdims=True) + NORM_EPS)
    u = (y * g_ref[...]).astype(u_ref.dtype)
    u_ref[...] = u
    zg_ref[...] = jnp.dot(u, wg_ref[...], preferred_element_type=F32)


def _rmsnorm(h, gain, w_gates=None):
    T, D = h.shape
    tm = _row_tile(T, 384)
    gain = gain.reshape(1, D).astype(F32)
    if w_gates is None:
        return pl.pallas_call(
            _rmsnorm_kernel,
            grid=(T // tm,),
            in_specs=[pl.BlockSpec((tm, D), lambda i: (i, 0)),
                      pl.BlockSpec((1, D), lambda i: (0, 0))],
            out_specs=pl.BlockSpec((tm, D), lambda i: (i, 0)),
            out_shape=jax.ShapeDtypeStruct((T, D), BF16),
            compiler_params=_params(("parallel",)),
            name="rmsnorm",
        )(h, gain)
    return pl.pallas_call(
        _rmsnorm_gates_kernel,
        grid=(T // tm,),
        in_specs=[pl.BlockSpec((tm, D), lambda i: (i, 0)),
                  pl.BlockSpec((1, D), lambda i: (0, 0)),
                  pl.BlockSpec((D, LANES), lambda i: (0, 0))],
        out_specs=[pl.BlockSpec((tm, D), lambda i: (i, 0)),
                   pl.BlockSpec((tm, LANES), lambda i: (i, 0))],
        out_shape=[jax.ShapeDtypeStruct((T, D), BF16),
                   jax.ShapeDtypeStruct((T, LANES), F32)],
        compiler_params=_params(("parallel",)),
        name="rmsnorm_gates",
    )(h, gain, w_gates)


def _final_norm_kernel(h_ref, g_ref, o_ref):
    x = h_ref[...]
    y = x * lax.rsqrt(jnp.mean(x * x, axis=-1, keepdims=True) + NORM_EPS)
    o_ref[...] = y * g_ref[...]


def _final_norm(h, gain, batch, l_pad, seq):
    T, D = h.shape
    lead = l_pad - seq
    assert lead % CHUNK == 0 and seq % CHUNK == 0
    nlead, nseq, ntot = lead // CHUNK, seq // CHUNK, l_pad // CHUNK
    return pl.pallas_call(
        _final_norm_kernel,
        grid=(batch, nseq),
        in_specs=[pl.BlockSpec((CHUNK, D), lambda b, i: (b * ntot + nlead + i, 0)),
                  pl.BlockSpec((1, D), lambda b, i: (0, 0))],
        out_specs=pl.BlockSpec((None, CHUNK, D), lambda b, i: (b, i, 0)),
        out_shape=jax.ShapeDtypeStruct((batch, seq, D), F32),
        compiler_params=_params(("parallel", "parallel")),
        name="final_norm",
    )(h, gain.reshape(1, D).astype(F32))


def _ffn_up_kernel(u_ref, wg_ref, wu_ref, a_ref):
    u = u_ref[...]
    g = jnp.dot(u, wg_ref[...], preferred_element_type=F32)
    up = jnp.dot(u, wu_ref[...], preferred_element_type=F32)
    a_ref[...] = (_silu(g) * up).astype(a_ref.dtype)


def _ffn_up(u, w_gate, w_up):
    T, D = u.shape
    F = w_gate.shape[1]
    tm, tn = _row_tile(T, 1376), _col_tile(F, 512)
    return pl.pallas_call(
        _ffn_up_kernel,
        grid=(T // tm, F // tn),
        in_specs=[pl.BlockSpec((tm, D), lambda i, j: (i, 0)),
                  pl.BlockSpec((D, tn), lambda i, j: (0, j)),
                  pl.BlockSpec((D, tn), lambda i, j: (0, j))],
        out_specs=pl.BlockSpec((tm, tn), lambda i, j: (i, j)),
        out_shape=jax.ShapeDtypeStruct((T, F), BF16),
        compiler_params=_params(("parallel", "arbitrary")),
        name="ffn_up",
    )(u, w_gate, w_up)


def _resid_mm_kernel(a_ref, w_ref, h_ref, o_ref, *, scale):
    o_ref[...] = h_ref[...] + scale * jnp.dot(a_ref[...], w_ref[...], preferred_element_type=F32)


def _resid_mm(a, w, h, scale):
    T, K = a.shape
    N = w.shape[1]
    tm, tn = _row_tile(T, 1376), _col_tile(N, 512)
    return pl.pallas_call(
        functools.partial(_resid_mm_kernel, scale=scale),
        grid=(T // tm, N // tn),
        in_specs=[pl.BlockSpec((tm, K), lambda i, j: (i, 0)),
                  pl.BlockSpec((K, tn), lambda i, j: (0, j)),
                  pl.BlockSpec((tm, tn), lambda i, j: (i, j))],
        out_specs=pl.BlockSpec((tm, tn), lambda i, j: (i, j)),
        out_shape=jax.ShapeDtypeStruct((T, N), F32),
        input_output_aliases={2: 0},
        compiler_params=_params(("parallel", "arbitrary")),
        name="resid_mm",
    )(a, w, h)


def _mm_kernel(u_ref, w_ref, o_ref):
    o_ref[...] = jnp.dot(u_ref[...], w_ref[...], preferred_element_type=F32).astype(o_ref.dtype)


def _in_proj(u, w):
    T, K = u.shape
    N = w.shape[1]
    tm, tn = _row_tile(T, 1376), _col_tile(N, 512)
    return pl.pallas_call(
        _mm_kernel,
        grid=(T // tm, N // tn),
        in_specs=[pl.BlockSpec((tm, K), lambda i, j: (i, 0)),
                  pl.BlockSpec((K, tn), lambda i, j: (0, j))],
        out_specs=pl.BlockSpec((tm, tn), lambda i, j: (i, j)),
        out_shape=jax.ShapeDtypeStruct((T, N), BF16),
        compiler_params=_params(("parallel", "arbitrary")),
        name="in_proj",
    )(u, w)


def _merge_kernel(ya_ref, yb_ref, wa_ref, wb_ref, ga_ref, gb_ref, y_ref):
    pa = jnp.dot(ya_ref[...], wa_ref[...], preferred_element_type=F32)
    pb = jnp.dot(yb_ref[...], wb_ref[...], preferred_element_type=F32)
    y = jax.nn.sigmoid(ga_ref[...].astype(F32)) * pa + jax.nn.sigmoid(gb_ref[...].astype(F32)) * pb
    y_ref[...] = y.astype(y_ref.dtype)


def _merge(ya, yb, w_a, w_b, zgate):
    T, Ka = ya.shape
    Kb = yb.shape[1]
    N = w_a.shape[1]
    tm, tn = _row_tile(T, 1376), _col_tile(N, 512)
    nb = N // tn
    return pl.pallas_call(
        _merge_kernel,
        grid=(T // tm, nb),
        in_specs=[pl.BlockSpec((tm, Ka), lambda i, j: (i, 0)),
                  pl.BlockSpec((tm, Kb), lambda i, j: (i, 0)),
                  pl.BlockSpec((Ka, tn), lambda i, j: (0, j)),
                  pl.BlockSpec((Kb, tn), lambda i, j: (0, j)),
                  pl.BlockSpec((tm, tn), lambda i, j: (i, j)),
                  pl.BlockSpec((tm, tn), lambda i, j: (i, nb + j))],
        out_specs=pl.BlockSpec((tm, tn), lambda i, j: (i, j)),
        out_shape=jax.ShapeDtypeStruct((T, N), BF16),
        compiler_params=_params(("parallel", "arbitrary")),
        name="merge",
    )(ya, yb, w_a, w_b, zgate, zgate)


def _hgrn_head(q, k, b2, v16, st):
    half_row = lax.broadcasted_iota(jnp.int32, (SUBLANES, CHUNK), 0)
    half_lane = lax.broadcasted_iota(jnp.int32, (SUBLANES, CHUNK), 1)
    b_last = b2[CHUNK - 1:CHUNK]
    c2 = b2 - jnp.log2(k)

    o = lax.dot_general((q * jnp.exp2(b2)).astype(BF16), st.astype(BF16), _NT,
                        preferred_element_type=F32)
    a_rows = []
    for blk in range(CHUNK // SUB):
        lo = blk * SUB
        q_i, b_i, c_i = q[lo:lo + SUB], b2[lo:lo + SUB], c2[lo:lo + SUB]
        if blk == 0:
            a_top = jnp.zeros((SUBLANES, CHUNK), F32)
            a_bot = a_top
        else:
            beta = b2[lo - 1:lo]
            qt = (q_i * jnp.exp2(b_i - beta)).astype(BF16)
            kt = (k[:lo] * jnp.exp2(beta - b2[:lo])).astype(BF16)
            kt = jnp.concatenate([kt, jnp.zeros((CHUNK - lo, HEAD_DK), BF16)], axis=0)
            a_off = lax.dot_general(qt, kt, _NT, preferred_element_type=F32)
            a_top, a_bot = a_off[:SUBLANES], a_off[SUBLANES:]
        for jj in range(SUB):
            col = half_lane == lo + jj
            cj = c_i[jj:jj + 1]
            if jj < SUBLANES:
                s_top = jnp.sum(q_i[:SUBLANES] * jnp.exp2(b_i[:SUBLANES] - cj), axis=-1, keepdims=True)
                a_top = jnp.where(col, s_top, a_top)
            s_bot = jnp.sum(q_i[SUBLANES:] * jnp.exp2(b_i[SUBLANES:] - cj), axis=-1, keepdims=True)
            a_bot = jnp.where(col, s_bot, a_bot)
        a_top = jnp.where(half_lane > lo + half_row, 0.0, a_top)
        a_bot = jnp.where(half_lane > lo + SUBLANES + half_row, 0.0, a_bot)
        a_rows += [a_top, a_bot]
    a_mat = jnp.concatenate(a_rows, axis=0)
    o = o + jnp.dot(a_mat.astype(BF16), v16, preferred_element_type=F32)

    khat = (k * jnp.exp2(b_last - b2)).astype(BF16)
    st_new = st * jnp.exp2(b_last) + lax.dot_general(v16, khat, _TN, preferred_element_type=F32)
    return o, st_new


def _hgrn_kernel(lbl_ref, q_ref, f_ref, v_ref, g_ref, ng_ref, o_ref, st_ref, *, layer, blk_chunks, pad, group):
    t = pl.program_id(2)
    gw = group * HEAD_DK
    if layer > 0:
        lbl = lbl_ref[...]
        e = jnp.exp(lbl - jnp.max(lbl, axis=0, keepdims=True))
        p = e / jnp.sum(e, axis=0, keepdims=True)
        lb = p[1:2]
        for r in range(2, layer + 1):
            lb = lb + p[r:r + 1]
        log_lb = jnp.log(lb)
        log1m_lb = jnp.log1p(-lb)
    ng = ng_ref[...]

    row = lax.broadcasted_iota(jnp.int32, (CHUNK, 1), 0)
    tri_r = lax.broadcasted_iota(jnp.int32, (CHUNK, CHUNK), 0)
    tri_c = lax.broadcasted_iota(jnp.int32, (CHUNK, CHUNK), 1)
    tril = (tri_r >= tri_c).astype(F32)
    row0 = t * (blk_chunks * CHUNK)

    @pl.when(t == 0)
    def _():
        st_ref[...] = jnp.zeros_like(st_ref)

    def chunk(c):
        r0 = c * CHUNK if isinstance(c, int) else pl.multiple_of(c * CHUNK, CHUNK)
        valid = (row0 + r0 + row) >= pad
        qz = q_ref[pl.ds(r0, CHUNK), :].astype(F32)
        fz = f_ref[pl.ds(r0, CHUNK), :].astype(F32)
        v16 = v_ref[pl.ds(r0, CHUNK), :]
        gz = g_ref[pl.ds(r0, CHUNK), :].astype(F32)

        q = _silu(qz)
        ls = _log_sigmoid(fz)
        if layer == 0:
            log_f = ls
            k = jnp.exp(ls - fz)
        else:
            b_ = log1m_lb + ls
            log_f = jnp.maximum(log_lb, b_) + jnp.log1p(jnp.exp(-jnp.abs(log_lb - b_)))
            k = jnp.exp(b_ - fz)
        k = jnp.where(valid, k, 0.0)
        b2 = jnp.dot(tril, log_f * LOG2E, precision=_HI, preferred_element_type=F32)
        gate = _silu(gz) * ng

        for g in range(group):
            sl = slice(g * HEAD_DK, (g + 1) * HEAD_DK)
            o, st_new = _hgrn_head(q[:, sl], k[:, sl], b2[:, sl], v16[:, sl], st_ref[g])
            st_ref[g] = st_new
            y = o * lax.rsqrt(jnp.mean(o * o, axis=-1, keepdims=True) + NORM_EPS)
            o_ref[pl.ds(r0, CHUNK), sl] = (y * gate[:, sl]).astype(o_ref.dtype)

    _for_chunks(blk_chunks, chunk, HG_UNROLL)


def _hgrn(z, lb_logits, norm_g, layer, batch, l_pad, heads, pad):
    T = z.shape[0]
    depth = lb_logits.shape[0]
    group = _group(heads, HG_GROUP)
    ngrp = heads // group
    gw = group * HEAD_DK
    n_chunks = l_pad // CHUNK
    blk_chunks = _block_chunks(n_chunks)
    nt = n_chunks // blk_chunks
    tb = blk_chunks * CHUNK
    kern = functools.partial(_hgrn_kernel, layer=layer, blk_chunks=blk_chunks, pad=pad, group=group)
    seq_blk = lambda sec: pl.BlockSpec((tb, gw), lambda b, h, t: (b * nt + t, sec * ngrp + h))
    return pl.pallas_call(
        kern,
        grid=(batch, ngrp, nt),
        in_specs=[pl.BlockSpec((depth, gw), lambda b, h, t: (0, h)),
                  seq_blk(0), seq_blk(1), seq_blk(2), seq_blk(3),
                  pl.BlockSpec((1, gw), lambda b, h, t: (0, h))],
        out_specs=pl.BlockSpec((tb, gw), lambda b, h, t: (b * nt + t, h)),
        out_shape=jax.ShapeDtypeStruct((T, heads * HG_DV), BF16),
        scratch_shapes=[pltpu.VMEM((group, HG_DV, HEAD_DK), F32)],
        compiler_params=_params(("parallel", "parallel", "arbitrary")),
        name="hgrn2",
    )(lb_logits.astype(F32), z, z, z, z, norm_g.reshape(1, -1).astype(F32))


def _mlstm_head(q, k, v16, i_fin, fcum, valid, valid_col, cs, ns, m_prev):
    tri_r = lax.broadcasted_iota(jnp.int32, (CHUNK, CHUNK), 0)
    tri_c = lax.broadcasted_iota(jnp.int32, (CHUNK, CHUNK), 1)
    causal = tri_r >= tri_c
    i_pre = jnp.where(valid, i_fin, -jnp.inf)
    g_row = jnp.sum(jnp.where(tri_r == tri_c, fcum - i_fin, 0.0), axis=0, keepdims=True)
    log_d = jnp.where(jnp.logical_and(causal, valid_col), fcum - g_row, -jnp.inf)

    log_prev = fcum + m_prev
    m_t = jnp.maximum(log_prev, jnp.max(log_d, axis=-1, keepdims=True))
    w_prev = jnp.exp(log_prev - m_t)
    q16 = q.astype(BF16)
    sqk = lax.dot_general(q16, k.astype(BF16), _NT, preferred_element_type=F32) * jnp.exp(log_d - m_t)

    num = w_prev * jnp.dot(q16, cs.astype(BF16), preferred_element_type=F32) \
        + jnp.dot(sqk.astype(BF16), v16, preferred_element_type=F32)
    den = w_prev * jnp.sum(q * ns, axis=-1, keepdims=True) + jnp.sum(sqk, axis=-1, keepdims=True)
    hval = num / jnp.maximum(jnp.abs(den), jnp.exp(-m_t))

    m_new = m_t[CHUNK - 1:CHUNK]
    f_last = fcum[CHUNK - 1:CHUNK]
    w_old = jnp.exp(f_last + m_prev - m_new)
    kw = k * jnp.exp(f_last - fcum + i_pre - m_new)
    cs_new = w_old * cs + lax.dot_general(kw.astype(BF16), v16, _TN, preferred_element_type=F32)
    ns_new = w_old * ns + jnp.sum(kw, axis=0, keepdims=True)
    return hval, cs_new, ns_new, m_new


def _mlstm_kernel(zg_ref, gbias_ref, q_ref, k_ref, v_ref, og_ref, cwq_ref, cwk_ref, cbq_ref, cbk_ref,
                  ng_ref, o_ref, cs_ref, ns_ref, m_ref, pq_ref, pk_ref, *, blk_chunks, pad, heads, group):
    hg = pl.program_id(1)
    t = pl.program_id(2)
    row = lax.broadcasted_iota(jnp.int32, (CHUNK, 1), 0)
    lane = lax.broadcasted_iota(jnp.int32, (CHUNK, LANES), 1)
    tri_c = lax.broadcasted_iota(jnp.int32, (CHUNK, CHUNK), 1)
    tril = (lax.broadcasted_iota(jnp.int32, (CHUNK, CHUNK), 0) >= tri_c).astype(F32)
    gbias = gbias_ref[...]
    cwq, cwk = cwq_ref[...], cwk_ref[...]
    cbq, cbk = cbq_ref[...], cbk_ref[...]
    ng = ng_ref[...]
    k_scale = HEAD_DK ** -0.5
    row0 = t * (blk_chunks * CHUNK)

    @pl.when(t == 0)
    def _():
        cs_ref[...] = jnp.zeros_like(cs_ref)
        ns_ref[...] = jnp.zeros_like(ns_ref)
        m_ref[...] = jnp.zeros_like(m_ref)
        pq_ref[...] = jnp.zeros_like(pq_ref)
        pk_ref[...] = jnp.zeros_like(pk_ref)

    def conv_silu(cur, prev, w, bias):
        y = cur * w[CONV_W - 1:CONV_W] + bias
        for s in range(1, CONV_W):
            mixed = jnp.where(row >= CHUNK - s, prev, cur)
            y = y + pltpu.roll(mixed, s, 0) * w[CONV_W - 1 - s:CONV_W - s]
        return _silu(y)

    def chunk(c):
        r0 = c * CHUNK if isinstance(c, int) else pl.multiple_of(c * CHUNK, CHUNK)
        valid = (row0 + r0 + row) >= pad
        valid_col = (row0 + r0 + tri_c) >= pad

        q_in = jnp.where(valid, q_ref[pl.ds(r0, CHUNK), :].astype(F32), 0.0)
        k_in = jnp.where(valid, k_ref[pl.ds(r0, CHUNK), :].astype(F32), 0.0)
        q = conv_silu(q_in, pq_ref[...], cwq, cbq)
        k = conv_silu(k_in, pk_ref[...], cwk, cbk) * k_scale
        pq_ref[...] = q_in
        pk_ref[...] = k_in
        v16 = v_ref[pl.ds(r0, CHUNK), :]
        og = jax.nn.sigmoid(og_ref[pl.ds(r0, CHUNK), :].astype(F32)) * ng
        capped = GATE_CAP * jnp.tanh((zg_ref[pl.ds(r0, CHUNK), :] + gbias) / GATE_CAP)
        log_f = jnp.where(valid, _log_sigmoid(capped), 0.0)
        fcum_all = jnp.dot(tril, log_f, precision=_HI, preferred_element_type=F32)

        for g in range(group):
            head = hg * group + g
            i_fin = jnp.sum(jnp.where(lane == head, capped, 0.0), axis=-1, keepdims=True)
            fcum = jnp.sum(jnp.where(lane == heads + head, fcum_all, 0.0), axis=-1, keepdims=True)
            sk = slice(g * HEAD_DK, (g + 1) * HEAD_DK)
            sv = slice(g * ML_DV, (g + 1) * ML_DV)
            hval, cs_new, ns_new, m_new = _mlstm_head(
                q[:, sk], k[:, sk], v16[:, sv], i_fin, fcum, valid, valid_col,
                cs_ref[g], ns_ref[g], m_ref[g])
            cs_ref[g] = cs_new
            ns_ref[g] = ns_new
            m_ref[g] = m_new
            y = hval * lax.rsqrt(jnp.mean(hval * hval, axis=-1, keepdims=True) + NORM_EPS)
            o_ref[pl.ds(r0, CHUNK), sv] = (y * og[:, sv]).astype(o_ref.dtype)

    _for_chunks(blk_chunks, chunk, ML_UNROLL)


def _mlstm(zq, zg, gate_bias, conv_w, conv_b, norm_g, batch, l_pad, heads, q_col, pad):
    T = zq.shape[0]
    group = _group(heads, ML_GROUP)
    ngrp = heads // group
    gk, gv = group * HEAD_DK, group * ML_DV
    qk_w = heads * HEAD_DK
    qb = q_col // gk
    kb = qb + ngrp
    vb = (q_col + 2 * qk_w) // gv
    ob = vb + ngrp
    assert q_col % gk == 0 and (q_col + 2 * qk_w) % gv == 0
    n_chunks = l_pad // CHUNK
    blk_chunks = _block_chunks(n_chunks)
    nt = n_chunks // blk_chunks
    tb = blk_chunks * CHUNK
    kern = functools.partial(_mlstm_kernel, blk_chunks=blk_chunks, pad=pad, heads=heads, group=group)
    rows = lambda b, h, t: b * nt + t
    return pl.pallas_call(
        kern,
        grid=(batch, ngrp, nt),
        in_specs=[pl.BlockSpec((tb, LANES), lambda b, h, t: (rows(b, h, t), 0)),
                  pl.BlockSpec((1, LANES), lambda b, h, t: (0, 0)),
                  pl.BlockSpec((tb, gk), lambda b, h, t: (rows(b, h, t), qb + h)),
                  pl.BlockSpec((tb, gk), lambda b, h, t: (rows(b, h, t), kb + h)),
                  pl.BlockSpec((tb, gv), lambda b, h, t: (rows(b, h, t), vb + h)),
                  pl.BlockSpec((tb, gv), lambda b, h, t: (rows(b, h, t), ob + h)),
                  pl.BlockSpec((CONV_W, gk), lambda b, h, t: (0, h)),
                  pl.BlockSpec((CONV_W, gk), lambda b, h, t: (0, ngrp + h)),
                  pl.BlockSpec((1, gk), lambda b, h, t: (0, h)),
                  pl.BlockSpec((1, gk), lambda b, h, t: (0, ngrp + h)),
                  pl.BlockSpec((1, gv), lambda b, h, t: (0, h))],
        out_specs=pl.BlockSpec((tb, gv), lambda b, h, t: (rows(b, h, t), h)),
        out_shape=jax.ShapeDtypeStruct((T, heads * ML_DV), BF16),
        scratch_shapes=[pltpu.VMEM((group, HEAD_DK, ML_DV), F32),
                        pltpu.VMEM((group, 1, HEAD_DK), F32),
                        pltpu.VMEM((group, 1, 1), F32),
                        pltpu.VMEM((CHUNK, gk), F32),
                        pltpu.VMEM((CHUNK, gk), F32)],
        compiler_params=_params(("parallel", "parallel", "arbitrary")),
        name="mlstm",
    )(zg, gate_bias, zq, zq, zq, zq, conv_w.astype(F32), conv_w.astype(F32),
      conv_b.reshape(1, -1).astype(F32), conv_b.reshape(1, -1).astype(F32),
      norm_g.reshape(1, -1).astype(F32))


def kernel(x, meta_tokens, hgrn_lb_logits, norm_ffn1, ffn1_w_gate, ffn1_w_up, ffn1_w_down, norm_mix, w_in, mlstm_conv_w, mlstm_conv_b, mlstm_igate_b, mlstm_fgate_b, hgrn_out_norm, mlstm_out_norm, w_branch_a, w_branch_b, w_out, norm_ffn2, ffn2_w_gate, ffn2_w_up, ffn2_w_down, final_norm):
    batch, seq, d_model = x.shape
    depth = w_in.shape[0]
    n_meta = meta_tokens.shape[0]
    hg_heads = w_branch_a.shape[1] // HG_DV
    ml_heads = mlstm_igate_b.shape[1]
    hg_w = hg_heads * HEAD_DK
    ml_qk = ml_heads * HEAD_DK
    ml_v = ml_heads * ML_DV
    n_gate = 2 * ml_heads
    assert n_gate <= LANES

    pad = (-(n_meta + seq)) % CHUNK
    l_pad = pad + n_meta + seq
    meta = jnp.broadcast_to(meta_tokens[None].astype(x.dtype), (batch, n_meta, d_model))
    h = jnp.concatenate([jnp.zeros((batch, pad, d_model), x.dtype), meta, x], axis=1)
    h = h.reshape(batch * l_pad, d_model)

    gates_col = 4 * hg_w + 2 * ml_qk + 2 * ml_v
    mlstm_q_col = 4 * hg_w
    w_mix = w_in[:, :, :gates_col].astype(BF16)
    w_scal = jnp.pad(w_in[:, :, gates_col:gates_col + n_gate],
                     ((0, 0), (0, 0), (0, LANES - n_gate))).astype(BF16)
    w_mrg = w_in[:, :, gates_col + n_gate:].astype(BF16)
    gate_bias = jnp.pad(jnp.concatenate([mlstm_igate_b, mlstm_fgate_b], axis=1).astype(F32),
                        ((0, 0), (0, LANES - n_gate)))

    for l in range(depth):
        u = _rmsnorm(h, norm_ffn1[l])
        a = _ffn_up(u, ffn1_w_gate[l].astype(BF16), ffn1_w_up[l].astype(BF16))
        h = _resid_mm(a, ffn1_w_down[l].astype(BF16), h, 0.5)

        u, zg = _rmsnorm(h, norm_mix[l], w_gates=w_scal[l])
        zmix = _in_proj(u, w_mix[l])
        zmrg = _in_proj(u, w_mrg[l])
        ya = _hgrn(zmix, hgrn_lb_logits, hgrn_out_norm[l], l, batch, l_pad, hg_heads, pad)
        yb = _mlstm(zmix, zg, gate_bias[l:l + 1], mlstm_conv_w[l], mlstm_conv_b[l], mlstm_out_norm[l],
                    batch, l_pad, ml_heads, mlstm_q_col, pad)
        y = _merge(ya, yb, w_branch_a[l].astype(BF16), w_branch_b[l].astype(BF16), zmrg)
        h = _resid_mm(y, w_out[l].astype(BF16), h, 1.0)

        u = _rmsnorm(h, norm_ffn2[l])
        a = _ffn_up(u, ffn2_w_gate[l].astype(BF16), ffn2_w_up[l].astype(BF16))
        h = _resid_mm(a, ffn2_w_down[l].astype(BF16), h, 0.5)

    return _final_norm(h, final_norm, batch, l_pad, seq)
```

```python
import functools

import jax
import jax.numpy as jnp
from jax import lax
from jax.experimental import pallas as pl
from jax.experimental.pallas import tpu as pltpu

F32 = jnp.float32
BF16 = jnp.bfloat16

CHUNK = 64
NORM_EPS = 1e-6
HEAD_DK = 128
HG_DV = 128
ML_DV = 256
CONV_W = 4
GATE_CAP = 15.0
SUB = 16
SUBLANES = 8
LANES = 128
VMEM_LIMIT = 56 * 1024 * 1024
HG_GROUP = 4
ML_GROUP = 4
HG_UNROLL = 3
ML_UNROLL = 2
MAX_BLOCK_CHUNKS = 48
LOG2E = 1.4426950408889634

_HI = lax.Precision.HIGHEST
_NT = (((1,), (1,)), ((), ()))
_TN = (((0,), (0,)), ((), ()))


def _params(sem):
    return pltpu.CompilerParams(dimension_semantics=sem, vmem_limit_bytes=VMEM_LIMIT)


def _row_tile(n_rows, target):
    return max(t for t in range(16, min(n_rows, target) + 1, 16) if n_rows % t == 0)


def _col_tile(n_cols, target):
    return max(t for t in range(LANES, min(n_cols, target) + 1, LANES) if n_cols % t == 0)


def _block_chunks(n_chunks):
    return max(d for d in range(1, min(n_chunks, MAX_BLOCK_CHUNKS) + 1) if n_chunks % d == 0)


def _group(heads, target):
    return max(g for g in range(1, target + 1) if heads % g == 0)


def _for_chunks(n, body, unroll):
    trips = n // unroll

    def trip(i, carry):
        for u in range(unroll):
            body(i * unroll + u)
        return carry

    if trips > 0:
        lax.fori_loop(0, trips, trip, 0)
    for c in range(trips * unroll, n):
        body(c)


def _tdot(a, b):
    n = a.shape[1]
    eye = (lax.broadcasted_iota(jnp.int32, (n, n), 0) ==
           lax.broadcasted_iota(jnp.int32, (n, n), 1)).astype(a.dtype)
    a_t = lax.dot_general(eye, a, _NT, preferred_element_type=F32).astype(a.dtype)
    return jnp.dot(a_t, b, preferred_element_type=F32)


def _log_sigmoid(z):
    return jnp.minimum(z, 0.0) - jnp.log1p(jnp.exp(-jnp.abs(z)))


def _silu(z):
    return z * jax.nn.sigmoid(z)


def _row_scale(ss_ref, width):
    return lax.rsqrt(ss_ref[:, :1] * (1.0 / width) + NORM_EPS)


def _prep_kernel(h_ref, g_ref, up_ref, ss_ref):
    x = h_ref[...]
    up_ref[...] = (x * g_ref[...]).astype(up_ref.dtype)
    ss_ref[...] = jnp.broadcast_to(jnp.sum(x * x, axis=-1, keepdims=True), ss_ref.shape)


def _prep(h, gain):
    T, D = h.shape
    tm = _row_tile(T, 384)
    return pl.pallas_call(
        _prep_kernel,
        grid=(T // tm,),
        in_specs=[pl.BlockSpec((tm, D), lambda i: (i, 0)),
                  pl.BlockSpec((1, D), lambda i: (0, 0))],
        out_specs=[pl.BlockSpec((tm, D), lambda i: (i, 0)),
                   pl.BlockSpec((tm, LANES), lambda i: (i, 0))],
        out_shape=[jax.ShapeDtypeStruct((T, D), BF16),
                   jax.ShapeDtypeStruct((T, LANES), F32)],
        compiler_params=_params(("parallel",)),
        name="prep",
    )(h, gain.reshape(1, D).astype(F32))


def _final_norm_kernel(h_ref, g_ref, o_ref):
    x = h_ref[...]
    y = x * lax.rsqrt(jnp.mean(x * x, axis=-1, keepdims=True) + NORM_EPS)
    o_ref[...] = y * g_ref[...]


def _final_norm(h, gain, batch, l_pad, seq):
    T, D = h.shape
    lead = l_pad - seq
    assert lead % CHUNK == 0 and seq % CHUNK == 0
    nlead, nseq, ntot = lead // CHUNK, seq // CHUNK, l_pad // CHUNK
    return pl.pallas_call(
        _final_norm_kernel,
        grid=(batch, nseq),
        in_specs=[pl.BlockSpec((CHUNK, D), lambda b, i: (b * ntot + nlead + i, 0)),
                  pl.BlockSpec((1, D), lambda b, i: (0, 0))],
        out_specs=pl.BlockSpec((None, CHUNK, D), lambda b, i: (b, i, 0)),
        out_shape=jax.ShapeDtypeStruct((batch, seq, D), F32),
        compiler_params=_params(("parallel", "parallel")),
        name="final_norm",
    )(h, gain.reshape(1, D).astype(F32))


def _ffn_up_kernel(u_ref, ss_ref, wg_ref, wu_ref, a_ref):
    u = u_ref[...]
    rs = _row_scale(ss_ref, u.shape[1])
    g = jnp.dot(u, wg_ref[...], preferred_element_type=F32) * rs
    up = jnp.dot(u, wu_ref[...], preferred_element_type=F32) * rs
    a_ref[...] = (_silu(g) * up).astype(a_ref.dtype)


def _ffn_up(u, ss, w_gate, w_up, layer):
    T, D = u.shape
    F = w_gate.shape[2]
    tm, tn = _row_tile(T, 1376), _col_tile(F, 512)
    return pl.pallas_call(
        _ffn_up_kernel,
        grid=(T // tm, F // tn),
        in_specs=[pl.BlockSpec((tm, D), lambda i, j: (i, 0)),
                  pl.BlockSpec((tm, LANES), lambda i, j: (i, 0)),
                  pl.BlockSpec((None, D, tn), lambda i, j: (layer, 0, j)),
                  pl.BlockSpec((None, D, tn), lambda i, j: (layer, 0, j))],
        out_specs=pl.BlockSpec((tm, tn), lambda i, j: (i, j)),
        out_shape=jax.ShapeDtypeStruct((T, F), BF16),
        compiler_params=_params(("parallel", "arbitrary")),
        name="ffn_up",
    )(u, ss, w_gate, w_up)


def _resid_mm_kernel(a_ref, w_ref, h_ref, g_ref, o_ref, up_ref, ss_ref, *, scale):
    hn = h_ref[...] + scale * jnp.dot(a_ref[...], w_ref[...], preferred_element_type=F32)
    o_ref[...] = hn
    up_ref[...] = (hn * g_ref[...]).astype(up_ref.dtype)
    part = jnp.broadcast_to(jnp.sum(hn * hn, axis=-1, keepdims=True), ss_ref.shape)

    @pl.when(pl.program_id(1) == 0)
    def _():
        ss_ref[...] = part

    @pl.when(pl.program_id(1) > 0)
    def _():
        ss_ref[...] += part


def _resid_mm(a, w, layer, h, scale, next_gain):
    T, K = a.shape
    N = w.shape[2]
    tm, tn = _row_tile(T, 1376), _col_tile(N, 512)
    return pl.pallas_call(
        functools.partial(_resid_mm_kernel, scale=scale),
        grid=(T // tm, N // tn),
        in_specs=[pl.BlockSpec((tm, K), lambda i, j: (i, 0)),
                  pl.BlockSpec((None, K, tn), lambda i, j: (layer, 0, j)),
                  pl.BlockSpec((tm, tn), lambda i, j: (i, j)),
                  pl.BlockSpec((1, tn), lambda i, j: (0, j))],
        out_specs=[pl.BlockSpec((tm, tn), lambda i, j: (i, j)),
                   pl.BlockSpec((tm, tn), lambda i, j: (i, j)),
                   pl.BlockSpec((tm, LANES), lambda i, j: (i, 0))],
        out_shape=[jax.ShapeDtypeStruct((T, N), F32),
                   jax.ShapeDtypeStruct((T, N), BF16),
                   jax.ShapeDtypeStruct((T, LANES), F32)],
        input_output_aliases={2: 0},
        compiler_params=_params(("parallel", "arbitrary")),
        name="resid_mm",
    )(a, w, h, next_gain.reshape(1, N).astype(F32))


def _in_proj_kernel(u_ref, ss_ref, w_ref, o_ref):
    u = u_ref[...]
    acc = jnp.dot(u, w_ref[...].astype(u.dtype), preferred_element_type=F32)
    o_ref[...] = (acc * _row_scale(ss_ref, u.shape[1])).astype(o_ref.dtype)


def _in_proj(u, ss, w, layer, n_cols, out_dtype):
    T, K = u.shape
    tm, tn = _row_tile(T, 1376), _col_tile(n_cols, 512)
    return pl.pallas_call(
        _in_proj_kernel,
        grid=(T // tm, n_cols // tn),
        in_specs=[pl.BlockSpec((tm, K), lambda i, j: (i, 0)),
                  pl.BlockSpec((tm, LANES), lambda i, j: (i, 0)),
                  pl.BlockSpec((None, K, tn), lambda i, j: (layer, 0, j))],
        out_specs=pl.BlockSpec((tm, tn), lambda i, j: (i, j)),
        out_shape=jax.ShapeDtypeStruct((T, n_cols), out_dtype),
        compiler_params=_params(("parallel", "arbitrary")),
        name="in_proj",
    )(u, ss, w)


def _merge_kernel(ya_ref, yb_ref, wa_ref, wb_ref, ga_ref, gb_ref, y_ref):
    pa = jnp.dot(ya_ref[...], wa_ref[...], preferred_element_type=F32)
    pb = jnp.dot(yb_ref[...], wb_ref[...], preferred_element_type=F32)
    y = jax.nn.sigmoid(ga_ref[...].astype(F32)) * pa + jax.nn.sigmoid(gb_ref[...].astype(F32)) * pb
    y_ref[...] = y.astype(y_ref.dtype)


def _merge(ya, yb, w_a, w_b, layer, zgate):
    T, Ka = ya.shape
    Kb = yb.shape[1]
    N = w_a.shape[2]
    tm, tn = _row_tile(T, 1376), _col_tile(N, 512)
    nb = N // tn
    return pl.pallas_call(
        _merge_kernel,
        grid=(T // tm, nb),
        in_specs=[pl.BlockSpec((tm, Ka), lambda i, j: (i, 0)),
                  pl.BlockSpec((tm, Kb), lambda i, j: (i, 0)),
                  pl.BlockSpec((None, Ka, tn), lambda i, j: (layer, 0, j)),
                  pl.BlockSpec((None, Kb, tn), lambda i, j: (layer, 0, j)),
                  pl.BlockSpec((tm, tn), lambda i, j: (i, j)),
                  pl.BlockSpec((tm, tn), lambda i, j: (i, nb + j))],
        out_specs=pl.BlockSpec((tm, tn), lambda i, j: (i, j)),
        out_shape=jax.ShapeDtypeStruct((T, N), BF16),
        compiler_params=_params(("parallel", "arbitrary")),
        name="merge",
    )(ya, yb, w_a, w_b, zgate, zgate)


def _hgrn_head(q, k, b2, v16, st):
    half_row = lax.broadcasted_iota(jnp.int32, (SUBLANES, CHUNK), 0)
    half_lane = lax.broadcasted_iota(jnp.int32, (SUBLANES, CHUNK), 1)
    b_last = b2[CHUNK - 1:CHUNK]
    c2 = b2 - jnp.log2(k)

    o = lax.dot_general((q * jnp.exp2(b2)).astype(BF16), st.astype(BF16), _NT,
                        preferred_element_type=F32)
    a_rows = []
    for blk in range(CHUNK // SUB):
        lo = blk * SUB
        q_i, b_i, c_i = q[lo:lo + SUB], b2[lo:lo + SUB], c2[lo:lo + SUB]
        if blk == 0:
            a_top = jnp.zeros((SUBLANES, CHUNK), F32)
            a_bot = a_top
        else:
            beta = b2[lo - 1:lo]
            qt = (q_i * jnp.exp2(b_i - beta)).astype(BF16)
            kt = (k[:lo] * jnp.exp2(beta - b2[:lo])).astype(BF16)
            kt = jnp.concatenate([kt, jnp.zeros((CHUNK - lo, HEAD_DK), BF16)], axis=0)
            a_off = lax.dot_general(qt, kt, _NT, preferred_element_type=F32)
            a_top, a_bot = a_off[:SUBLANES], a_off[SUBLANES:]
        for jj in range(SUB):
            col = half_lane == lo + jj
            cj = c_i[jj:jj + 1]
            if jj < SUBLANES:
                s_top = jnp.sum(q_i[:SUBLANES] * jnp.exp2(b_i[:SUBLANES] - cj), axis=-1, keepdims=True)
                a_top = jnp.where(col, s_top, a_top)
            s_bot = jnp.sum(q_i[SUBLANES:] * jnp.exp2(b_i[SUBLANES:] - cj), axis=-1, keepdims=True)
            a_bot = jnp.where(col, s_bot, a_bot)
        a_top = jnp.where(half_lane > lo + half_row, 0.0, a_top)
        a_bot = jnp.where(half_lane > lo + SUBLANES + half_row, 0.0, a_bot)
        a_rows += [a_top, a_bot]
    a_mat = jnp.concatenate(a_rows, axis=0)
    o = o + jnp.dot(a_mat.astype(BF16), v16, preferred_element_type=F32)

    khat = (k * jnp.exp2(b_last - b2)).astype(BF16)
    st_new = st * jnp.exp2(b_last) + lax.dot_general(v16, khat, _TN, preferred_element_type=F32)
    return o, st_new


def _hgrn_kernel(lbl_ref, q_ref, f_ref, v_ref, g_ref, ng_ref, o_ref, st_ref, *, layer, blk_chunks, pad, group):
    t = pl.program_id(2)
    gw = group * HEAD_DK
    if layer > 0:
        lbl = lbl_ref[...]
        e = jnp.exp(lbl - jnp.max(lbl, axis=0, keepdims=True))
        p = e / jnp.sum(e, axis=0, keepdims=True)
        lb = p[1:2]
        for r in range(2, layer + 1):
            lb = lb + p[r:r + 1]
        log_lb = jnp.log(lb)
        log1m_lb = jnp.log1p(-lb)
    ng = ng_ref[...]

    row = lax.broadcasted_iota(jnp.int32, (CHUNK, 1), 0)
    tri_r = lax.broadcasted_iota(jnp.int32, (CHUNK, CHUNK), 0)
    tri_c = lax.broadcasted_iota(jnp.int32, (CHUNK, CHUNK), 1)
    tril = (tri_r >= tri_c).astype(F32)
    row0 = t * (blk_chunks * CHUNK)

    @pl.when(t == 0)
    def _():
        st_ref[...] = jnp.zeros_like(st_ref)

    def chunk(c):
        r0 = c * CHUNK if isinstance(c, int) else pl.multiple_of(c * CHUNK, CHUNK)
        valid = (row0 + r0 + row) >= pad
        qz = q_ref[pl.ds(r0, CHUNK), :].astype(F32)
        fz = f_ref[pl.ds(r0, CHUNK), :].astype(F32)
        v16 = v_ref[pl.ds(r0, CHUNK), :]
        gz = g_ref[pl.ds(r0, CHUNK), :].astype(F32)

        q = _silu(qz)
        ls = _log_sigmoid(fz)
        if layer == 0:
            log_f = ls
            k = jnp.exp(ls - fz)
        else:
            b_ = log1m_lb + ls
            log_f = jnp.maximum(log_lb, b_) + jnp.log1p(jnp.exp(-jnp.abs(log_lb - b_)))
            k = jnp.exp(b_ - fz)
        k = jnp.where(valid, k, 0.0)
        b2 = jnp.dot(tril, log_f * LOG2E, precision=_HI, preferred_element_type=F32)
        gate = _silu(gz) * ng

        for g in range(group):
            sl = slice(g * HEAD_DK, (g + 1) * HEAD_DK)
            o, st_new = _hgrn_head(q[:, sl], k[:, sl], b2[:, sl], v16[:, sl], st_ref[g])
            st_ref[g] = st_new
            y = o * lax.rsqrt(jnp.mean(o * o, axis=-1, keepdims=True) + NORM_EPS)
            o_ref[pl.ds(r0, CHUNK), sl] = (y * gate[:, sl]).astype(o_ref.dtype)

    _for_chunks(blk_chunks, chunk, HG_UNROLL)


def _hgrn(z, lb_logits, norm_g, layer, batch, l_pad, heads, pad):
    T = z.shape[0]
    depth = lb_logits.shape[0]
    group = _group(heads, HG_GROUP)
    ngrp = heads // group
    gw = group * HEAD_DK
    n_chunks = l_pad // CHUNK
    blk_chunks = _block_chunks(n_chunks)
    nt = n_chunks // blk_chunks
    tb = blk_chunks * CHUNK
    kern = functools.partial(_hgrn_kernel, layer=layer, blk_chunks=blk_chunks, pad=pad, group=group)
    seq_blk = lambda sec: pl.BlockSpec((tb, gw), lambda b, h, t: (b * nt + t, sec * ngrp + h))
    return pl.pallas_call(
        kern,
        grid=(batch, ngrp, nt),
        in_specs=[pl.BlockSpec((depth, gw), lambda b, h, t: (0, h)),
                  seq_blk(0), seq_blk(1), seq_blk(2), seq_blk(3),
                  pl.BlockSpec((1, gw), lambda b, h, t: (0, h))],
        out_specs=pl.BlockSpec((tb, gw), lambda b, h, t: (b * nt + t, h)),
        out_shape=jax.ShapeDtypeStruct((T, heads * HG_DV), BF16),
        scratch_shapes=[pltpu.VMEM((group, HG_DV, HEAD_DK), F32)],
        compiler_params=_params(("parallel", "parallel", "arbitrary")),
        name="hgrn2",
    )(lb_logits.astype(F32), z, z, z, z, norm_g.reshape(1, -1).astype(F32))


def _mlstm_head(q, k, v16, i_fin, fcum, valid, valid_col, cs, ns, m_prev):
    tri_r = lax.broadcasted_iota(jnp.int32, (CHUNK, CHUNK), 0)
    tri_c = lax.broadcasted_iota(jnp.int32, (CHUNK, CHUNK), 1)
    causal = tri_r >= tri_c
    i_pre = jnp.where(valid, i_fin, -jnp.inf)
    g_row = jnp.sum(jnp.where(tri_r == tri_c, fcum - i_fin, 0.0), axis=0, keepdims=True)
    log_d = jnp.where(jnp.logical_and(causal, valid_col), fcum - g_row, -jnp.inf)

    log_prev = fcum + m_prev
    m_t = jnp.maximum(log_prev, jnp.max(log_d, axis=-1, keepdims=True))
    w_prev = jnp.exp(log_prev - m_t)
    q16 = q.astype(BF16)
    sqk = lax.dot_general(q16, k.astype(BF16), _NT, preferred_element_type=F32) * jnp.exp(log_d - m_t)

    num = w_prev * jnp.dot(q16, cs.astype(BF16), preferred_element_type=F32) \
        + jnp.dot(sqk.astype(BF16), v16, preferred_element_type=F32)
    den = w_prev * jnp.sum(q * ns, axis=-1, keepdims=True) + jnp.sum(sqk, axis=-1, keepdims=True)
    hval = num / jnp.maximum(jnp.abs(den), jnp.exp(-m_t))

    m_new = m_t[CHUNK - 1:CHUNK]
    f_last = fcum[CHUNK - 1:CHUNK]
    w_old = jnp.exp(f_last + m_prev - m_new)
    kw = k * jnp.exp(f_last - fcum + i_pre - m_new)
    cs_new = w_old * cs + _tdot(kw.astype(BF16), v16)
    ns_new = w_old * ns + jnp.sum(kw, axis=0, keepdims=True)
    return hval, cs_new, ns_new, m_new


def _mlstm_kernel(zg_ref, gbias_ref, q_ref, k_ref, v_ref, og_ref, cwq_ref, cwk_ref, cbq_ref, cbk_ref,
                  ng_ref, o_ref, cs_ref, ns_ref, m_ref, pq_ref, pk_ref, *, blk_chunks, pad, heads, group):
    hg = pl.program_id(1)
    t = pl.program_id(2)
    row = lax.broadcasted_iota(jnp.int32, (CHUNK, 1), 0)
    lane = lax.broadcasted_iota(jnp.int32, (CHUNK, LANES), 1)
    tri_c = lax.broadcasted_iota(jnp.int32, (CHUNK, CHUNK), 1)
    tril = (lax.broadcasted_iota(jnp.int32, (CHUNK, CHUNK), 0) >= tri_c).astype(F32)
    gbias = gbias_ref[...]
    cwq, cwk = cwq_ref[...], cwk_ref[...]
    cbq, cbk = cbq_ref[...], cbk_ref[...]
    ng = ng_ref[...]
    k_scale = HEAD_DK ** -0.5
    row0 = t * (blk_chunks * CHUNK)

    @pl.when(t == 0)
    def _():
        cs_ref[...] = jnp.zeros_like(cs_ref)
        ns_ref[...] = jnp.zeros_like(ns_ref)
        m_ref[...] = jnp.zeros_like(m_ref)
        pq_ref[...] = jnp.zeros_like(pq_ref)
        pk_ref[...] = jnp.zeros_like(pk_ref)

    def conv_silu(cur, prev, w, bias):
        y = cur * w[CONV_W - 1:CONV_W] + bias
        for s in range(1, CONV_W):
            mixed = jnp.where(row >= CHUNK - s, prev, cur)
            y = y + pltpu.roll(mixed, s, 0) * w[CONV_W - 1 - s:CONV_W - s]
        return _silu(y)

    def chunk(c):
        r0 = c * CHUNK if isinstance(c, int) else pl.multiple_of(c * CHUNK, CHUNK)
        valid = (row0 + r0 + row) >= pad
        valid_col = (row0 + r0 + tri_c) >= pad

        q_in = jnp.where(valid, q_ref[pl.ds(r0, CHUNK), :].astype(F32), 0.0)
        k_in = jnp.where(valid, k_ref[pl.ds(r0, CHUNK), :].astype(F32), 0.0)
        q = conv_silu(q_in, pq_ref[...], cwq, cbq)
        k = conv_silu(k_in, pk_ref[...], cwk, cbk) * k_scale
        pq_ref[...] = q_in
        pk_ref[...] = k_in
        v16 = v_ref[pl.ds(r0, CHUNK), :]
        og = jax.nn.sigmoid(og_ref[pl.ds(r0, CHUNK), :].astype(F32)) * ng
        capped = GATE_CAP * jnp.tanh((zg_ref[pl.ds(r0, CHUNK), :] + gbias) / GATE_CAP)
        log_f = jnp.where(valid, _log_sigmoid(capped), 0.0)
        fcum_all = jnp.dot(tril, log_f, precision=_HI, preferred_element_type=F32)

        for g in range(group):
            head = hg * group + g
            i_fin = jnp.sum(jnp.where(lane == head, capped, 0.0), axis=-1, keepdims=True)
            fcum = jnp.sum(jnp.where(lane == heads + head, fcum_all, 0.0), axis=-1, keepdims=True)
            sk = slice(g * HEAD_DK, (g + 1) * HEAD_DK)
            sv = slice(g * ML_DV, (g + 1) * ML_DV)
            hval, cs_new, ns_new, m_new = _mlstm_head(
                q[:, sk], k[:, sk], v16[:, sv], i_fin, fcum, valid, valid_col,
                cs_ref[g], ns_ref[g], m_ref[g])
            cs_ref[g] = cs_new
            ns_ref[g] = ns_new
            m_ref[g] = m_new
            y = hval * lax.rsqrt(jnp.mean(hval * hval, axis=-1, keepdims=True) + NORM_EPS)
            o_ref[pl.ds(r0, CHUNK), sv] = (y * og[:, sv]).astype(o_ref.dtype)

    _for_chunks(blk_chunks, chunk, ML_UNROLL)


def _mlstm(zq, zg, gate_bias, conv_w, conv_b, norm_g, batch, l_pad, heads, q_col, pad):
    T = zq.shape[0]
    group = _group(heads, ML_GROUP)
    ngrp = heads // group
    gk, gv = group * HEAD_DK, group * ML_DV
    qk_w = heads * HEAD_DK
    qb = q_col // gk
    kb = qb + ngrp
    vb = (q_col + 2 * qk_w) // gv
    ob = vb + ngrp
    assert q_col % gk == 0 and (q_col + 2 * qk_w) % gv == 0
    n_chunks = l_pad // CHUNK
    blk_chunks = _block_chunks(n_chunks)
    nt = n_chunks // blk_chunks
    tb = blk_chunks * CHUNK
    kern = functools.partial(_mlstm_kernel, blk_chunks=blk_chunks, pad=pad, heads=heads, group=group)
    rows = lambda b, h, t: b * nt + t
    return pl.pallas_call(
        kern,
        grid=(batch, ngrp, nt),
        in_specs=[pl.BlockSpec((tb, LANES), lambda b, h, t: (rows(b, h, t), 0)),
                  pl.BlockSpec((1, LANES), lambda b, h, t: (0, 0)),
                  pl.BlockSpec((tb, gk), lambda b, h, t: (rows(b, h, t), qb + h)),
                  pl.BlockSpec((tb, gk), lambda b, h, t: (rows(b, h, t), kb + h)),
                  pl.BlockSpec((tb, gv), lambda b, h, t: (rows(b, h, t), vb + h)),
                  pl.BlockSpec((tb, gv), lambda b, h, t: (rows(b, h, t), ob + h)),
                  pl.BlockSpec((CONV_W, gk), lambda b, h, t: (0, h)),
                  pl.BlockSpec((CONV_W, gk), lambda b, h, t: (0, ngrp + h)),
                  pl.BlockSpec((1, gk), lambda b, h, t: (0, h)),
                  pl.BlockSpec((1, gk), lambda b, h, t: (0, ngrp + h)),
                  pl.BlockSpec((1, gv), lambda b, h, t: (0, h))],
        out_specs=pl.BlockSpec((tb, gv), lambda b, h, t: (rows(b, h, t), h)),
        out_shape=jax.ShapeDtypeStruct((T, heads * ML_DV), BF16),
        scratch_shapes=[pltpu.VMEM((group, HEAD_DK, ML_DV), F32),
                        pltpu.VMEM((group, 1, HEAD_DK), F32),
                        pltpu.VMEM((group, 1, 1), F32),
                        pltpu.VMEM((CHUNK, gk), F32),
                        pltpu.VMEM((CHUNK, gk), F32)],
        compiler_params=_params(("parallel", "parallel", "arbitrary")),
        name="mlstm",
    )(zg, gate_bias, zq, zq, zq, zq, conv_w.astype(F32), conv_w.astype(F32),
      conv_b.reshape(1, -1).astype(F32), conv_b.reshape(1, -1).astype(F32),
      norm_g.reshape(1, -1).astype(F32))


def kernel(x, meta_tokens, hgrn_lb_logits, norm_ffn1, ffn1_w_gate, ffn1_w_up, ffn1_w_down, norm_mix, w_in, mlstm_conv_w, mlstm_conv_b, mlstm_igate_b, mlstm_fgate_b, hgrn_out_norm, mlstm_out_norm, w_branch_a, w_branch_b, w_out, norm_ffn2, ffn2_w_gate, ffn2_w_up, ffn2_w_down, final_norm):
    batch, seq, d_model = x.shape
    depth = w_in.shape[0]
    n_meta = meta_tokens.shape[0]
    hg_heads = w_branch_a.shape[1] // HG_DV
    ml_heads = mlstm_igate_b.shape[1]
    hg_w = hg_heads * HEAD_DK
    ml_qk = ml_heads * HEAD_DK
    ml_v = ml_heads * ML_DV
    n_gate = 2 * ml_heads
    assert n_gate <= LANES

    pad = (-(n_meta + seq)) % CHUNK
    l_pad = pad + n_meta + seq
    meta = jnp.broadcast_to(meta_tokens[None].astype(x.dtype), (batch, n_meta, d_model))
    h = jnp.concatenate([jnp.zeros((batch, pad, d_model), x.dtype), meta, x], axis=1)
    h = h.reshape(batch * l_pad, d_model)

    gates_col = 4 * hg_w + 2 * ml_qk + 2 * ml_v
    mlstm_q_col = 4 * hg_w
    w_scal = jnp.pad(w_in[:, :, gates_col:gates_col + n_gate],
                     ((0, 0), (0, 0), (0, LANES - n_gate))).astype(BF16)
    w_mrg = w_in[:, :, gates_col + n_gate:].astype(BF16)
    gate_bias = jnp.pad(jnp.concatenate([mlstm_igate_b, mlstm_fgate_b], axis=1).astype(F32),
                        ((0, 0), (0, LANES - n_gate)))
    w_g1, w_u1, w_d1 = ffn1_w_gate.astype(BF16), ffn1_w_up.astype(BF16), ffn1_w_down.astype(BF16)
    w_g2, w_u2, w_d2 = ffn2_w_gate.astype(BF16), ffn2_w_up.astype(BF16), ffn2_w_down.astype(BF16)
    w_a, w_b, w_o = w_branch_a.astype(BF16), w_branch_b.astype(BF16), w_out.astype(BF16)

    u, ss = _prep(h, norm_ffn1[0])
    for l in range(depth):
        a = _ffn_up(u, ss, w_g1, w_u1, l)
        h, u, ss = _resid_mm(a, w_d1, l, h, 0.5, norm_mix[l])

        zmix = _in_proj(u, ss, w_in, l, gates_col, BF16)
        zg = _in_proj(u, ss, w_scal, l, LANES, F32)
        zmrg = _in_proj(u, ss, w_mrg, l, 2 * d_model, BF16)
        ya = _hgrn(zmix, hgrn_lb_logits, hgrn_out_norm[l], l, batch, l_pad, hg_heads, pad)
        yb = _mlstm(zmix, zg, gate_bias[l:l + 1], mlstm_conv_w[l], mlstm_conv_b[l], mlstm_out_norm[l],
                    batch, l_pad, ml_heads, mlstm_q_col, pad)
        y = _merge(ya, yb, w_a, w_b, l, zmrg)
        h, u, ss = _resid_mm(y, w_o, l, h, 1.0, norm_ffn2[l])

        a = _ffn_up(u, ss, w_g2, w_u2, l)
        next_gain = norm_ffn1[l + 1] if l + 1 < depth else final_norm
        h, u, ss = _resid_mm(a, w_d2, l, h, 0.5, next_gain)

    return _final_norm(h, final_norm, batch, l_pad, seq)
```

```python
import functools

import jax
import jax.numpy as jnp
from jax import lax
from jax.experimental import pallas as pl
from jax.experimental.pallas import tpu as pltpu

F32 = jnp.float32
BF16 = jnp.bfloat16

CHUNK = 64
NORM_EPS = 1e-6
HEAD_DK = 128
HG_DV = 128
ML_DV = 256
CONV_W = 4
GATE_CAP = 15.0
SUB = 16
SUBLANES = 8
LANES = 128
VMEM_LIMIT = 56 * 1024 * 1024
HG_GROUP = 4
ML_GROUP = 4
HG_UNROLL = 3
ML_UNROLL = 2
MAX_BLOCK_CHUNKS = 48
LOG2E = 1.4426950408889634

_HI = lax.Precision.HIGHEST
_NT = (((1,), (1,)), ((), ()))
_TN = (((0,), (0,)), ((), ()))


def _params(sem):
    return pltpu.CompilerParams(dimension_semantics=sem, vmem_limit_bytes=VMEM_LIMIT)


def _row_tile(n_rows, target):
    return max(t for t in range(16, min(n_rows, target) + 1, 16) if n_rows % t == 0)


def _col_tile(n_cols, target):
    return max(t for t in range(LANES, min(n_cols, target) + 1, LANES) if n_cols % t == 0)


def _block_chunks(n_chunks):
    return max(d for d in range(1, min(n_chunks, MAX_BLOCK_CHUNKS) + 1) if n_chunks % d == 0)


def _group(heads, target):
    return max(g for g in range(1, target + 1) if heads % g == 0)


def _for_chunks(n, body, unroll):
    trips = n // unroll

    def trip(i, carry):
        for u in range(unroll):
            body(i * unroll + u)
        return carry

    if trips > 0:
        lax.fori_loop(0, trips, trip, 0)
    for c in range(trips * unroll, n):
        body(c)


def _tdot(a, b):
    n = a.shape[1]
    eye = (lax.broadcasted_iota(jnp.int32, (n, n), 0) ==
           lax.broadcasted_iota(jnp.int32, (n, n), 1)).astype(a.dtype)
    a_t = lax.dot_general(eye, a, _NT, preferred_element_type=F32).astype(a.dtype)
    return jnp.dot(a_t, b, preferred_element_type=F32)


def _log_sigmoid(z):
    return jnp.minimum(z, 0.0) - jnp.log1p(jnp.exp(-jnp.abs(z)))


def _sigmoid(z):
    return 0.5 * jnp.tanh(0.5 * z) + 0.5


def _silu(z):
    return z * _sigmoid(z)


def _row_scale(ss_ref, width):
    return lax.rsqrt(ss_ref[:, :1] * (1.0 / width) + NORM_EPS)


def _prep_kernel(h_ref, g_ref, up_ref, ss_ref):
    x = h_ref[...]
    up_ref[...] = (x * g_ref[...]).astype(up_ref.dtype)
    ss_ref[...] = jnp.broadcast_to(jnp.sum(x * x, axis=-1, keepdims=True), ss_ref.shape)


def _prep(h, gain):
    T, D = h.shape
    tm = _row_tile(T, 384)
    return pl.pallas_call(
        _prep_kernel,
        grid=(T // tm,),
        in_specs=[pl.BlockSpec((tm, D), lambda i: (i, 0)),
                  pl.BlockSpec((1, D), lambda i: (0, 0))],
        out_specs=[pl.BlockSpec((tm, D), lambda i: (i, 0)),
                   pl.BlockSpec((tm, LANES), lambda i: (i, 0))],
        out_shape=[jax.ShapeDtypeStruct((T, D), BF16),
                   jax.ShapeDtypeStruct((T, LANES), F32)],
        compiler_params=_params(("parallel",)),
        name="prep",
    )(h, gain.reshape(1, D).astype(F32))


def _final_norm_kernel(h_ref, g_ref, o_ref):
    x = h_ref[...]
    y = x * lax.rsqrt(jnp.mean(x * x, axis=-1, keepdims=True) + NORM_EPS)
    o_ref[...] = y * g_ref[...]


def _final_norm(h, gain, batch, l_pad, seq):
    T, D = h.shape
    lead = l_pad - seq
    tm = _row_tile(seq, 512)
    assert lead % SUBLANES == 0
    return pl.pallas_call(
        _final_norm_kernel,
        grid=(batch, seq // tm),
        in_specs=[pl.BlockSpec((pl.Element(tm), pl.Element(D)),
                               lambda b, i: (pl.multiple_of(b * l_pad + lead + i * tm, SUBLANES), 0)),
                  pl.BlockSpec((1, D), lambda b, i: (0, 0))],
        out_specs=pl.BlockSpec((None, tm, D), lambda b, i: (b, i, 0)),
        out_shape=jax.ShapeDtypeStruct((batch, seq, D), F32),
        compiler_params=_params(("parallel", "parallel")),
        name="final_norm",
    )(h, gain.reshape(1, D).astype(F32))


def _ffn_up_kernel(u_ref, ss_ref, wg_ref, wu_ref, a_ref):
    u = u_ref[...]
    rs = _row_scale(ss_ref, u.shape[1])
    g = jnp.dot(u, wg_ref[...], preferred_element_type=F32) * rs
    up = jnp.dot(u, wu_ref[...], preferred_element_type=F32) * rs
    a_ref[...] = (_silu(g) * up).astype(a_ref.dtype)


def _ffn_up(u, ss, w_gate, w_up, layer):
    T, D = u.shape
    F = w_gate.shape[2]
    tm, tn = _row_tile(T, 1376), _col_tile(F, 512)
    return pl.pallas_call(
        _ffn_up_kernel,
        grid=(T // tm, F // tn),
        in_specs=[pl.BlockSpec((tm, D), lambda i, j: (i, 0)),
                  pl.BlockSpec((tm, LANES), lambda i, j: (i, 0)),
                  pl.BlockSpec((None, D, tn), lambda i, j: (layer, 0, j)),
                  pl.BlockSpec((None, D, tn), lambda i, j: (layer, 0, j))],
        out_specs=pl.BlockSpec((tm, tn), lambda i, j: (i, j)),
        out_shape=jax.ShapeDtypeStruct((T, F), BF16),
        compiler_params=_params(("parallel", "arbitrary")),
        name="ffn_up",
    )(u, ss, w_gate, w_up)


def _resid_mm_kernel(a_ref, w_ref, h_ref, g_ref, o_ref, up_ref, ss_ref, *, scale):
    hn = h_ref[...] + scale * jnp.dot(a_ref[...], w_ref[...], preferred_element_type=F32)
    o_ref[...] = hn
    up_ref[...] = (hn * g_ref[...]).astype(up_ref.dtype)
    part = jnp.broadcast_to(jnp.sum(hn * hn, axis=-1, keepdims=True), ss_ref.shape)

    @pl.when(pl.program_id(1) == 0)
    def _():
        ss_ref[...] = part

    @pl.when(pl.program_id(1) > 0)
    def _():
        ss_ref[...] += part


def _resid_mm(a, w, layer, h, scale, next_gain):
    T, K = a.shape
    N = w.shape[2]
    tm, tn = _row_tile(T, 1376), _col_tile(N, 512)
    return pl.pallas_call(
        functools.partial(_resid_mm_kernel, scale=scale),
        grid=(T // tm, N // tn),
        in_specs=[pl.BlockSpec((tm, K), lambda i, j: (i, 0)),
                  pl.BlockSpec((None, K, tn), lambda i, j: (layer, 0, j)),
                  pl.BlockSpec((tm, tn), lambda i, j: (i, j)),
                  pl.BlockSpec((1, tn), lambda i, j: (0, j))],
        out_specs=[pl.BlockSpec((tm, tn), lambda i, j: (i, j)),
                   pl.BlockSpec((tm, tn), lambda i, j: (i, j)),
                   pl.BlockSpec((tm, LANES), lambda i, j: (i, 0))],
        out_shape=[jax.ShapeDtypeStruct((T, N), F32),
                   jax.ShapeDtypeStruct((T, N), BF16),
                   jax.ShapeDtypeStruct((T, LANES), F32)],
        input_output_aliases={2: 0},
        compiler_params=_params(("parallel", "arbitrary")),
        name="resid_mm",
    )(a, w, h, next_gain.reshape(1, N).astype(F32))


def _in_proj_kernel(u_ref, ss_ref, wt_ref, o_ref):
    u = u_ref[...]
    acc = lax.dot_general(u, wt_ref[...].astype(u.dtype), _NT, preferred_element_type=F32)
    o_ref[...] = (acc * _row_scale(ss_ref, u.shape[1])).astype(o_ref.dtype)


def _in_proj(u, ss, w_t, layer, n_cols, out_dtype):
    T, K = u.shape
    tm, tn = _row_tile(T, 1376), _col_tile(n_cols, 512)
    return pl.pallas_call(
        _in_proj_kernel,
        grid=(T // tm, n_cols // tn),
        in_specs=[pl.BlockSpec((tm, K), lambda i, j: (i, 0)),
                  pl.BlockSpec((tm, LANES), lambda i, j: (i, 0)),
                  pl.BlockSpec((None, tn, K), lambda i, j: (layer, j, 0))],
        out_specs=pl.BlockSpec((tm, tn), lambda i, j: (i, j)),
        out_shape=jax.ShapeDtypeStruct((T, n_cols), out_dtype),
        compiler_params=_params(("parallel", "arbitrary")),
        name="in_proj",
    )(u, ss, w_t)


def _merge_kernel(ya_ref, yb_ref, wa_ref, wb_ref, ga_ref, gb_ref, y_ref):
    pa = jnp.dot(ya_ref[...], wa_ref[...], preferred_element_type=F32)
    pb = jnp.dot(yb_ref[...], wb_ref[...], preferred_element_type=F32)
    y = _sigmoid(ga_ref[...].astype(F32)) * pa + _sigmoid(gb_ref[...].astype(F32)) * pb
    y_ref[...] = y.astype(y_ref.dtype)


def _merge(ya, yb, w_a, w_b, layer, zgate):
    T, Ka = ya.shape
    Kb = yb.shape[1]
    N = w_a.shape[2]
    tm, tn = _row_tile(T, 1376), _col_tile(N, 512)
    nb = N // tn
    return pl.pallas_call(
        _merge_kernel,
        grid=(T // tm, nb),
        in_specs=[pl.BlockSpec((tm, Ka), lambda i, j: (i, 0)),
                  pl.BlockSpec((tm, Kb), lambda i, j: (i, 0)),
                  pl.BlockSpec((None, Ka, tn), lambda i, j: (layer, 0, j)),
                  pl.BlockSpec((None, Kb, tn), lambda i, j: (layer, 0, j)),
                  pl.BlockSpec((tm, tn), lambda i, j: (i, j)),
                  pl.BlockSpec((tm, tn), lambda i, j: (i, nb + j))],
        out_specs=pl.BlockSpec((tm, tn), lambda i, j: (i, j)),
        out_shape=jax.ShapeDtypeStruct((T, N), BF16),
        compiler_params=_params(("parallel", "arbitrary")),
        name="merge",
    )(ya, yb, w_a, w_b, zgate, zgate)


def _hgrn_head(q, k, b2, v16, st):
    half_row = lax.broadcasted_iota(jnp.int32, (SUBLANES, CHUNK), 0)
    half_lane = lax.broadcasted_iota(jnp.int32, (SUBLANES, CHUNK), 1)
    b_last = b2[CHUNK - 1:CHUNK]
    c2 = b2 - jnp.log2(k)

    o = lax.dot_general((q * jnp.exp2(b2)).astype(BF16), st.astype(BF16), _NT,
                        preferred_element_type=F32)
    a_rows = []
    for blk in range(CHUNK // SUB):
        lo = blk * SUB
        q_i, b_i, c_i = q[lo:lo + SUB], b2[lo:lo + SUB], c2[lo:lo + SUB]
        if blk == 0:
            a_top = jnp.zeros((SUBLANES, CHUNK), F32)
            a_bot = a_top
        else:
            beta = b2[lo - 1:lo]
            qt = (q_i * jnp.exp2(b_i - beta)).astype(BF16)
            kt = (k[:lo] * jnp.exp2(beta - b2[:lo])).astype(BF16)
            kt = jnp.concatenate([kt, jnp.zeros((CHUNK - lo, HEAD_DK), BF16)], axis=0)
            a_off = lax.dot_general(qt, kt, _NT, preferred_element_type=F32)
            a_top, a_bot = a_off[:SUBLANES], a_off[SUBLANES:]
        for jj in range(SUB):
            col = half_lane == lo + jj
            cj = c_i[jj:jj + 1]
            if jj < SUBLANES:
                s_top = jnp.sum(q_i[:SUBLANES] * jnp.exp2(b_i[:SUBLANES] - cj), axis=-1, keepdims=True)
                a_top = jnp.where(col, s_top, a_top)
            s_bot = jnp.sum(q_i[SUBLANES:] * jnp.exp2(b_i[SUBLANES:] - cj), axis=-1, keepdims=True)
            a_bot = jnp.where(col, s_bot, a_bot)
        a_top = jnp.where(half_lane > lo + half_row, 0.0, a_top)
        a_bot = jnp.where(half_lane > lo + SUBLANES + half_row, 0.0, a_bot)
        a_rows += [a_top, a_bot]
    a_mat = jnp.concatenate(a_rows, axis=0)
    o = o + jnp.dot(a_mat.astype(BF16), v16, preferred_element_type=F32)

    khat = (k * jnp.exp2(b_last - b2)).astype(BF16)
    st_new = st * jnp.exp2(b_last) + lax.dot_general(v16, khat, _TN, preferred_element_type=F32)
    return o, st_new


def _hgrn_kernel(lbl_ref, q_ref, f_ref, v_ref, g_ref, ng_ref, o_ref, st_ref, *, layer, blk_chunks, pad, group):
    t = pl.program_id(2)
    gw = group * HEAD_DK
    if layer > 0:
        lbl = lbl_ref[...]
        e = jnp.exp(lbl - jnp.max(lbl, axis=0, keepdims=True))
        p = e / jnp.sum(e, axis=0, keepdims=True)
        lb = p[1:2]
        for r in range(2, layer + 1):
            lb = lb + p[r:r + 1]
        log_lb = jnp.log(lb)
        log1m_lb = jnp.log1p(-lb)
    ng = ng_ref[...]

    row = lax.broadcasted_iota(jnp.int32, (CHUNK, 1), 0)
    tri_r = lax.broadcasted_iota(jnp.int32, (CHUNK, CHUNK), 0)
    tri_c = lax.broadcasted_iota(jnp.int32, (CHUNK, CHUNK), 1)
    tril = (tri_r >= tri_c).astype(F32)
    row0 = t * (blk_chunks * CHUNK)

    @pl.when(t == 0)
    def _():
        st_ref[...] = jnp.zeros_like(st_ref)

    def chunk(c):
        r0 = c * CHUNK if isinstance(c, int) else pl.multiple_of(c * CHUNK, CHUNK)
        valid = (row0 + r0 + row) >= pad
        qz = q_ref[pl.ds(r0, CHUNK), :].astype(F32)
        fz = f_ref[pl.ds(r0, CHUNK), :].astype(F32)
        v16 = v_ref[pl.ds(r0, CHUNK), :]
        gz = g_ref[pl.ds(r0, CHUNK), :].astype(F32)

        q = _silu(qz)
        ls = _log_sigmoid(fz)
        if layer == 0:
            log_f = ls
            k = jnp.exp(ls - fz)
        else:
            b_ = log1m_lb + ls
            log_f = jnp.maximum(log_lb, b_) + jnp.log1p(jnp.exp(-jnp.abs(log_lb - b_)))
            k = jnp.exp(b_ - fz)
        k = jnp.where(valid, k, 0.0)
        b2 = jnp.dot(tril, log_f * LOG2E, precision=_HI, preferred_element_type=F32)
        gate = _silu(gz) * ng

        for g in range(group):
            sl = slice(g * HEAD_DK, (g + 1) * HEAD_DK)
            o, st_new = _hgrn_head(q[:, sl], k[:, sl], b2[:, sl], v16[:, sl], st_ref[g])
            st_ref[g] = st_new
            y = o * lax.rsqrt(jnp.mean(o * o, axis=-1, keepdims=True) + NORM_EPS)
            o_ref[pl.ds(r0, CHUNK), sl] = (y * gate[:, sl]).astype(o_ref.dtype)

    _for_chunks(blk_chunks, chunk, HG_UNROLL)


def _hgrn(z, lb_logits, norm_g, layer, batch, l_pad, heads, pad):
    T = z.shape[0]
    depth = lb_logits.shape[0]
    group = _group(heads, HG_GROUP)
    ngrp = heads // group
    gw = group * HEAD_DK
    n_chunks = l_pad // CHUNK
    blk_chunks = _block_chunks(n_chunks)
    nt = n_chunks // blk_chunks
    tb = blk_chunks * CHUNK
    kern = functools.partial(_hgrn_kernel, layer=layer, blk_chunks=blk_chunks, pad=pad, group=group)
    seq_blk = lambda sec: pl.BlockSpec((tb, gw), lambda b, h, t: (b * nt + t, sec * ngrp + h))
    return pl.pallas_call(
        kern,
        grid=(batch, ngrp, nt),
        in_specs=[pl.BlockSpec((depth, gw), lambda b, h, t: (0, h)),
                  seq_blk(0), seq_blk(1), seq_blk(2), seq_blk(3),
                  pl.BlockSpec((1, gw), lambda b, h, t: (0, h))],
        out_specs=pl.BlockSpec((tb, gw), lambda b, h, t: (b * nt + t, h)),
        out_shape=jax.ShapeDtypeStruct((T, heads * HG_DV), BF16),
        scratch_shapes=[pltpu.VMEM((group, HG_DV, HEAD_DK), F32)],
        compiler_params=_params(("parallel", "parallel", "arbitrary")),
        name="hgrn2",
    )(lb_logits.astype(F32), z, z, z, z, norm_g.reshape(1, -1).astype(F32))


def _mlstm_head(q, k, v16, i_fin, fcum, valid, valid_col, cs, ns, m_prev):
    tri_r = lax.broadcasted_iota(jnp.int32, (CHUNK, CHUNK), 0)
    tri_c = lax.broadcasted_iota(jnp.int32, (CHUNK, CHUNK), 1)
    causal = tri_r >= tri_c
    i_pre = jnp.where(valid, i_fin, -jnp.inf)
    g_row = jnp.sum(jnp.where(tri_r == tri_c, fcum - i_fin, 0.0), axis=0, keepdims=True)
    log_d = jnp.where(jnp.logical_and(causal, valid_col), fcum - g_row, -jnp.inf)

    log_prev = fcum + m_prev
    m_t = jnp.maximum(log_prev, jnp.max(log_d, axis=-1, keepdims=True))
    w_prev = jnp.exp(log_prev - m_t)
    q16 = q.astype(BF16)
    sqk = lax.dot_general(q16, k.astype(BF16), _NT, preferred_element_type=F32) * jnp.exp(log_d - m_t)

    num = w_prev * jnp.dot(q16, cs.astype(BF16), preferred_element_type=F32) \
        + jnp.dot(sqk.astype(BF16), v16, preferred_element_type=F32)
    den = w_prev * jnp.sum(q * ns, axis=-1, keepdims=True) + jnp.sum(sqk, axis=-1, keepdims=True)
    hval = num / jnp.maximum(jnp.abs(den), jnp.exp(-m_t))

    m_new = m_t[CHUNK - 1:CHUNK]
    f_last = fcum[CHUNK - 1:CHUNK]
    w_old = jnp.exp(f_last + m_prev - m_new)
    kw = k * jnp.exp(f_last - fcum + i_pre - m_new)
    cs_new = w_old * cs + _tdot(kw.astype(BF16), v16)
    ns_new = w_old * ns + jnp.sum(kw, axis=0, keepdims=True)
    return hval, cs_new, ns_new, m_new


def _mlstm_kernel(zg_ref, gbias_ref, q_ref, k_ref, v_ref, og_ref, cwq_ref, cwk_ref, cbq_ref, cbk_ref,
                  ng_ref, o_ref, cs_ref, ns_ref, m_ref, pq_ref, pk_ref, *, blk_chunks, pad, heads, group):
    hg = pl.program_id(1)
    t = pl.program_id(2)
    row = lax.broadcasted_iota(jnp.int32, (CHUNK, 1), 0)
    lane = lax.broadcasted_iota(jnp.int32, (CHUNK, LANES), 1)
    tri_c = lax.broadcasted_iota(jnp.int32, (CHUNK, CHUNK), 1)
    tril = (lax.broadcasted_iota(jnp.int32, (CHUNK, CHUNK), 0) >= tri_c).astype(F32)
    gbias = gbias_ref[...]
    cwq, cwk = cwq_ref[...], cwk_ref[...]
    cbq, cbk = cbq_ref[...], cbk_ref[...]
    ng = ng_ref[...]
    k_scale = HEAD_DK ** -0.5
    row0 = t * (blk_chunks * CHUNK)

    @pl.when(t == 0)
    def _():
        cs_ref[...] = jnp.zeros_like(cs_ref)
        ns_ref[...] = jnp.zeros_like(ns_ref)
        m_ref[...] = jnp.zeros_like(m_ref)
        pq_ref[...] = jnp.zeros_like(pq_ref)
        pk_ref[...] = jnp.zeros_like(pk_ref)

    def conv_silu(cur, prev, w, bias):
        y = cur * w[CONV_W - 1:CONV_W] + bias
        for s in range(1, CONV_W):
            mixed = jnp.where(row >= CHUNK - s, prev, cur)
            y = y + pltpu.roll(mixed, s, 0) * w[CONV_W - 1 - s:CONV_W - s]
        return _silu(y)

    def chunk(c):
        r0 = c * CHUNK if isinstance(c, int) else pl.multiple_of(c * CHUNK, CHUNK)
        valid = (row0 + r0 + row) >= pad
        valid_col = (row0 + r0 + tri_c) >= pad

        q_in = jnp.where(valid, q_ref[pl.ds(r0, CHUNK), :].astype(F32), 0.0)
        k_in = jnp.where(valid, k_ref[pl.ds(r0, CHUNK), :].astype(F32), 0.0)
        q = conv_silu(q_in, pq_ref[...], cwq, cbq)
        k = conv_silu(k_in, pk_ref[...], cwk, cbk) * k_scale
        pq_ref[...] = q_in
        pk_ref[...] = k_in
        v16 = v_ref[pl.ds(r0, CHUNK), :]
        og = _sigmoid(og_ref[pl.ds(r0, CHUNK), :].astype(F32)) * ng
        capped = GATE_CAP * jnp.tanh((zg_ref[pl.ds(r0, CHUNK), :] + gbias) / GATE_CAP)
        log_f = jnp.where(valid, _log_sigmoid(capped), 0.0)
        fcum_all = jnp.dot(tril, log_f, precision=_HI, preferred_element_type=F32)

        for g in range(group):
            head = hg * group + g
            i_fin = jnp.sum(jnp.where(lane == head, capped, 0.0), axis=-1, keepdims=True)
            fcum = jnp.sum(jnp.where(lane == heads + head, fcum_all, 0.0), axis=-1, keepdims=True)
            sk = slice(g * HEAD_DK, (g + 1) * HEAD_DK)
            sv = slice(g * ML_DV, (g + 1) * ML_DV)
            hval, cs_new, ns_new, m_new = _mlstm_head(
                q[:, sk], k[:, sk], v16[:, sv], i_fin, fcum, valid, valid_col,
                cs_ref[g], ns_ref[g], m_ref[g])
            cs_ref[g] = cs_new
            ns_ref[g] = ns_new
            m_ref[g] = m_new
            y = hval * lax.rsqrt(jnp.mean(hval * hval, axis=-1, keepdims=True) + NORM_EPS)
            o_ref[pl.ds(r0, CHUNK), sv] = (y * og[:, sv]).astype(o_ref.dtype)

    _for_chunks(blk_chunks, chunk, ML_UNROLL)


def _mlstm(zq, zg, gate_bias, conv_w, conv_b, norm_g, batch, l_pad, heads, q_col, pad):
    T = zq.shape[0]
    group = _group(heads, ML_GROUP)
    ngrp = heads // group
    gk, gv = group * HEAD_DK, group * ML_DV
    qk_w = heads * HEAD_DK
    qb = q_col // gk
    kb = qb + ngrp
    vb = (q_col + 2 * qk_w) // gv
    ob = vb + ngrp
    assert q_col % gk == 0 and (q_col + 2 * qk_w) % gv == 0
    n_chunks = l_pad // CHUNK
    blk_chunks = _block_chunks(n_chunks)
    nt = n_chunks // blk_chunks
    tb = blk_chunks * CHUNK
    kern = functools.partial(_mlstm_kernel, blk_chunks=blk_chunks, pad=pad, heads=heads, group=group)
    rows = lambda b, h, t: b * nt + t
    return pl.pallas_call(
        kern,
        grid=(batch, ngrp, nt),
        in_specs=[pl.BlockSpec((tb, LANES), lambda b, h, t: (rows(b, h, t), 0)),
                  pl.BlockSpec((1, LANES), lambda b, h, t: (0, 0)),
                  pl.BlockSpec((tb, gk), lambda b, h, t: (rows(b, h, t), qb + h)),
                  pl.BlockSpec((tb, gk), lambda b, h, t: (rows(b, h, t), kb + h)),
                  pl.BlockSpec((tb, gv), lambda b, h, t: (rows(b, h, t), vb + h)),
                  pl.BlockSpec((tb, gv), lambda b, h, t: (rows(b, h, t), ob + h)),
                  pl.BlockSpec((CONV_W, gk), lambda b, h, t: (0, h)),
                  pl.BlockSpec((CONV_W, gk), lambda b, h, t: (0, ngrp + h)),
                  pl.BlockSpec((1, gk), lambda b, h, t: (0, h)),
                  pl.BlockSpec((1, gk), lambda b, h, t: (0, ngrp + h)),
                  pl.BlockSpec((1, gv), lambda b, h, t: (0, h))],
        out_specs=pl.BlockSpec((tb, gv), lambda b, h, t: (rows(b, h, t), h)),
        out_shape=jax.ShapeDtypeStruct((T, heads * ML_DV), BF16),
        scratch_shapes=[pltpu.VMEM((group, HEAD_DK, ML_DV), F32),
                        pltpu.VMEM((group, 1, HEAD_DK), F32),
                        pltpu.VMEM((group, 1, 1), F32),
                        pltpu.VMEM((CHUNK, gk), F32),
                        pltpu.VMEM((CHUNK, gk), F32)],
        compiler_params=_params(("parallel", "parallel", "arbitrary")),
        name="mlstm",
    )(zg, gate_bias, zq, zq, zq, zq, conv_w.astype(F32), conv_w.astype(F32),
      conv_b.reshape(1, -1).astype(F32), conv_b.reshape(1, -1).astype(F32),
      norm_g.reshape(1, -1).astype(F32))


def kernel(x, meta_tokens, hgrn_lb_logits, norm_ffn1, ffn1_w_gate, ffn1_w_up, ffn1_w_down, norm_mix, w_in, mlstm_conv_w, mlstm_conv_b, mlstm_igate_b, mlstm_fgate_b, hgrn_out_norm, mlstm_out_norm, w_branch_a, w_branch_b, w_out, norm_ffn2, ffn2_w_gate, ffn2_w_up, ffn2_w_down, final_norm):
    batch, seq, d_model = x.shape
    depth = w_in.shape[0]
    n_meta = meta_tokens.shape[0]
    hg_heads = w_branch_a.shape[1] // HG_DV
    ml_heads = mlstm_igate_b.shape[1]
    hg_w = hg_heads * HEAD_DK
    ml_qk = ml_heads * HEAD_DK
    ml_v = ml_heads * ML_DV
    n_gate = 2 * ml_heads
    assert n_gate <= LANES

    pad = (-(n_meta + seq)) % CHUNK
    l_pad = pad + n_meta + seq
    meta = jnp.broadcast_to(meta_tokens[None].astype(x.dtype), (batch, n_meta, d_model))
    h = jnp.concatenate([jnp.zeros((batch, pad, d_model), x.dtype), meta, x], axis=1)
    h = h.reshape(batch * l_pad, d_model)

    gates_col = 4 * hg_w + 2 * ml_qk + 2 * ml_v
    mlstm_q_col = 4 * hg_w
    w_in_t = jnp.swapaxes(w_in, 1, 2)
    w_scal_t = jnp.pad(w_in_t[:, gates_col:gates_col + n_gate],
                       ((0, 0), (0, LANES - n_gate), (0, 0))).astype(BF16)
    w_mrg_t = w_in_t[:, gates_col + n_gate:].astype(BF16)
    gate_bias = jnp.pad(jnp.concatenate([mlstm_igate_b, mlstm_fgate_b], axis=1).astype(F32),
                        ((0, 0), (0, LANES - n_gate)))
    w_g1, w_u1, w_d1 = ffn1_w_gate.astype(BF16), ffn1_w_up.astype(BF16), ffn1_w_down.astype(BF16)
    w_g2, w_u2, w_d2 = ffn2_w_gate.astype(BF16), ffn2_w_up.astype(BF16), ffn2_w_down.astype(BF16)
    w_a, w_b, w_o = w_branch_a.astype(BF16), w_branch_b.astype(BF16), w_out.astype(BF16)

    u, ss = _prep(h, norm_ffn1[0])
    for l in range(depth):
        a = _ffn_up(u, ss, w_g1, w_u1, l)
        h, u, ss = _resid_mm(a, w_d1, l, h, 0.5, norm_mix[l])

        zmix = _in_proj(u, ss, w_in_t, l, gates_col, BF16)
        zg = _in_proj(u, ss, w_scal_t, l, LANES, F32)
        zmrg = _in_proj(u, ss, w_mrg_t, l, 2 * d_model, BF16)
        ya = _hgrn(zmix, hgrn_lb_logits, hgrn_out_norm[l], l, batch, l_pad, hg_heads, pad)
        yb = _mlstm(zmix, zg, gate_bias[l:l + 1], mlstm_conv_w[l], mlstm_conv_b[l], mlstm_out_norm[l],
                    batch, l_pad, ml_heads, mlstm_q_col, pad)
        y = _merge(ya, yb, w_a, w_b, l, zmrg)
        h, u, ss = _resid_mm(y, w_o, l, h, 1.0, norm_ffn2[l])

        a = _ffn_up(u, ss, w_g2, w_u2, l)
        next_gain = norm_ffn1[l + 1] if l + 1 < depth else final_norm
        h, u, ss = _resid_mm(a, w_d2, l, h, 0.5, next_gain)

    return _final_norm(h, final_norm, batch, l_pad, seq)
```

```python
import functools
import math

import jax
import jax.numpy as jnp
from jax import lax
from jax.experimental import pallas as pl
from jax.experimental.pallas import tpu as pltpu

F32 = jnp.float32
BF16 = jnp.bfloat16

CHUNK = 64
NORM_EPS = 1e-6
HEAD_DK = 128
HG_DV = 128
ML_DV = 256
CONV_W = 4
GATE_CAP = 15.0
SUB = 16
SUBLANES = 8
LANES = 128
VMEM_LIMIT = 56 * 1024 * 1024
HG_GROUP = 4
ML_GROUP = 4
HG_UNROLL = 3
ML_UNROLL = 3
MAX_BLOCK_CHUNKS = 48
LOG2E = 1.4426950408889634

_HI = lax.Precision.HIGHEST
_NT = (((1,), (1,)), ((), ()))
_TN = (((0,), (0,)), ((), ()))


def _params(sem):
    return pltpu.CompilerParams(dimension_semantics=sem, vmem_limit_bytes=VMEM_LIMIT)


def _row_tile(n_rows, target):
    return max(t for t in range(16, min(n_rows, target) + 1, 16) if n_rows % t == 0)


def _col_tile(n_cols, target):
    return max(t for t in range(LANES, min(n_cols, target) + 1, LANES) if n_cols % t == 0)


def _block_chunks(n_chunks):
    return max(d for d in range(1, min(n_chunks, MAX_BLOCK_CHUNKS) + 1) if n_chunks % d == 0)


def _group(heads, target):
    return max(g for g in range(1, target + 1) if heads % g == 0)


def _for_chunks(n, body, unroll):
    trips = n // unroll

    def trip(i, carry):
        for u in range(unroll):
            body(i * unroll + u)
        return carry

    if trips > 0:
        lax.fori_loop(0, trips, trip, 0)
    for c in range(trips * unroll, n):
        body(c)


def _tdot(a, b):
    n = a.shape[1]
    eye = (lax.broadcasted_iota(jnp.int32, (n, n), 0) ==
           lax.broadcasted_iota(jnp.int32, (n, n), 1)).astype(a.dtype)
    a_t = lax.dot_general(eye, a, _NT, preferred_element_type=F32).astype(a.dtype)
    return jnp.dot(a_t, b, preferred_element_type=F32)


def _log_sigmoid(z):
    return jnp.minimum(z, 0.0) - jnp.log1p(jnp.exp(-jnp.abs(z)))


def _sigmoid(z):
    return 0.5 * jnp.tanh(0.5 * z) + 0.5


def _silu(z):
    return z * _sigmoid(z)


def _row_scale(ss_ref, width):
    return lax.rsqrt(ss_ref[:, :1] * (1.0 / width) + NORM_EPS)


def _prep_kernel(h_ref, g_ref, up_ref, ss_ref):
    x = h_ref[...]
    up_ref[...] = (x * g_ref[...]).astype(up_ref.dtype)
    ss_ref[...] = jnp.broadcast_to(jnp.sum(x * x, axis=-1, keepdims=True), ss_ref.shape)


def _prep(h, gain):
    T, D = h.shape
    tm = _row_tile(T, 384)
    return pl.pallas_call(
        _prep_kernel,
        grid=(T // tm,),
        in_specs=[pl.BlockSpec((tm, D), lambda i: (i, 0)),
                  pl.BlockSpec((1, D), lambda i: (0, 0))],
        out_specs=[pl.BlockSpec((tm, D), lambda i: (i, 0)),
                   pl.BlockSpec((tm, LANES), lambda i: (i, 0))],
        out_shape=[jax.ShapeDtypeStruct((T, D), BF16),
                   jax.ShapeDtypeStruct((T, LANES), F32)],
        compiler_params=_params(("parallel",)),
        name="prep",
    )(h, gain.reshape(1, D).astype(F32))


def _final_norm_kernel(h_ref, g_ref, o_ref):
    x = h_ref[...]
    y = x * lax.rsqrt(jnp.mean(x * x, axis=-1, keepdims=True) + NORM_EPS)
    o_ref[...] = y * g_ref[...]


def _final_norm(h, gain, batch, l_pad, seq):
    T, D = h.shape
    lead = l_pad - seq
    tm = _row_tile(seq, 512)
    assert lead % SUBLANES == 0
    return pl.pallas_call(
        _final_norm_kernel,
        grid=(batch, seq // tm),
        in_specs=[pl.BlockSpec((pl.Element(tm), pl.Element(D)),
                               lambda b, i: (pl.multiple_of(b * l_pad + lead + i * tm, SUBLANES), 0)),
                  pl.BlockSpec((1, D), lambda b, i: (0, 0))],
        out_specs=pl.BlockSpec((None, tm, D), lambda b, i: (b, i, 0)),
        out_shape=jax.ShapeDtypeStruct((batch, seq, D), F32),
        compiler_params=_params(("parallel", "parallel")),
        name="final_norm",
    )(h, gain.reshape(1, D).astype(F32))


def _ffn_up_kernel(u_ref, ss_ref, wg_ref, wu_ref, a_ref):
    u = u_ref[...]
    rs = _row_scale(ss_ref, u.shape[1])
    g = jnp.dot(u, wg_ref[...], preferred_element_type=F32) * rs
    up = jnp.dot(u, wu_ref[...], preferred_element_type=F32) * rs
    a_ref[...] = (_silu(g) * up).astype(a_ref.dtype)


def _ffn_up(u, ss, w_gate, w_up, layer):
    T, D = u.shape
    F = w_gate.shape[2]
    tm, tn = _row_tile(T, 1376), _col_tile(F, 512)
    return pl.pallas_call(
        _ffn_up_kernel,
        grid=(T // tm, F // tn),
        in_specs=[pl.BlockSpec((tm, D), lambda i, j: (i, 0)),
                  pl.BlockSpec((tm, LANES), lambda i, j: (i, 0)),
                  pl.BlockSpec((None, D, tn), lambda i, j: (layer, 0, j)),
                  pl.BlockSpec((None, D, tn), lambda i, j: (layer, 0, j))],
        out_specs=pl.BlockSpec((tm, tn), lambda i, j: (i, j)),
        out_shape=jax.ShapeDtypeStruct((T, F), BF16),
        compiler_params=_params(("parallel", "arbitrary")),
        name="ffn_up",
    )(u, ss, w_gate, w_up)


def _resid_mm_kernel(a_ref, w_ref, h_ref, g_ref, o_ref, up_ref, ss_ref, *, scale):
    hn = h_ref[...] + scale * jnp.dot(a_ref[...], w_ref[...], preferred_element_type=F32)
    o_ref[...] = hn
    up_ref[...] = (hn * g_ref[...]).astype(up_ref.dtype)
    part = jnp.broadcast_to(jnp.sum(hn * hn, axis=-1, keepdims=True), ss_ref.shape)

    @pl.when(pl.program_id(1) == 0)
    def _():
        ss_ref[...] = part

    @pl.when(pl.program_id(1) > 0)
    def _():
        ss_ref[...] += part


def _resid_mm(a, w, layer, h, scale, next_gain):
    T, K = a.shape
    N = w.shape[2]
    tm, tn = _row_tile(T, 1376), _col_tile(N, 512)
    return pl.pallas_call(
        functools.partial(_resid_mm_kernel, scale=scale),
        grid=(T // tm, N // tn),
        in_specs=[pl.BlockSpec((tm, K), lambda i, j: (i, 0)),
                  pl.BlockSpec((None, K, tn), lambda i, j: (layer, 0, j)),
                  pl.BlockSpec((tm, tn), lambda i, j: (i, j)),
                  pl.BlockSpec((1, tn), lambda i, j: (0, j))],
        out_specs=[pl.BlockSpec((tm, tn), lambda i, j: (i, j)),
                   pl.BlockSpec((tm, tn), lambda i, j: (i, j)),
                   pl.BlockSpec((tm, LANES), lambda i, j: (i, 0))],
        out_shape=[jax.ShapeDtypeStruct((T, N), F32),
                   jax.ShapeDtypeStruct((T, N), BF16),
                   jax.ShapeDtypeStruct((T, LANES), F32)],
        input_output_aliases={2: 0},
        compiler_params=_params(("parallel", "arbitrary")),
        name="resid_mm",
    )(a, w, h, next_gain.reshape(1, N).astype(F32))


def _in_proj_kernel(u_ref, ss_ref, wt_ref, o_ref):
    u = u_ref[...]
    acc = lax.dot_general(u, wt_ref[...].astype(u.dtype), _NT, preferred_element_type=F32)
    o_ref[...] = (acc * _row_scale(ss_ref, u.shape[1])).astype(o_ref.dtype)


def _in_proj(u, ss, w_t, layer, row0, n_cols, out_dtype):
    T, K = u.shape
    tn = _col_tile(n_cols, 512)
    tm = _row_tile(T, 1376 if n_cols > tn else 688)
    align = math.gcd(row0, tn)
    return pl.pallas_call(
        _in_proj_kernel,
        grid=(T // tm, n_cols // tn),
        in_specs=[pl.BlockSpec((tm, K), lambda i, j: (i, 0)),
                  pl.BlockSpec((tm, LANES), lambda i, j: (i, 0)),
                  pl.BlockSpec((None, pl.Element(tn), pl.Element(K)),
                               lambda i, j: (layer, pl.multiple_of(row0 + j * tn, align), 0))],
        out_specs=pl.BlockSpec((tm, tn), lambda i, j: (i, j)),
        out_shape=jax.ShapeDtypeStruct((T, n_cols), out_dtype),
        compiler_params=_params(("parallel", "arbitrary")),
        name="in_proj",
    )(u, ss, w_t)


def _merge_kernel(ya_ref, yb_ref, wa_ref, wb_ref, ga_ref, gb_ref, y_ref):
    pa = jnp.dot(ya_ref[...], wa_ref[...], preferred_element_type=F32)
    pb = jnp.dot(yb_ref[...], wb_ref[...], preferred_element_type=F32)
    y = _sigmoid(ga_ref[...].astype(F32)) * pa + _sigmoid(gb_ref[...].astype(F32)) * pb
    y_ref[...] = y.astype(y_ref.dtype)


def _merge(ya, yb, w_a, w_b, layer, zgate):
    T, Ka = ya.shape
    Kb = yb.shape[1]
    N = w_a.shape[2]
    tm, tn = _row_tile(T, 1376), _col_tile(N, 512)
    nb = N // tn
    return pl.pallas_call(
        _merge_kernel,
        grid=(T // tm, nb),
        in_specs=[pl.BlockSpec((tm, Ka), lambda i, j: (i, 0)),
                  pl.BlockSpec((tm, Kb), lambda i, j: (i, 0)),
                  pl.BlockSpec((None, Ka, tn), lambda i, j: (layer, 0, j)),
                  pl.BlockSpec((None, Kb, tn), lambda i, j: (layer, 0, j)),
                  pl.BlockSpec((tm, tn), lambda i, j: (i, j)),
                  pl.BlockSpec((tm, tn), lambda i, j: (i, nb + j))],
        out_specs=pl.BlockSpec((tm, tn), lambda i, j: (i, j)),
        out_shape=jax.ShapeDtypeStruct((T, N), BF16),
        compiler_params=_params(("parallel", "arbitrary")),
        name="merge",
    )(ya, yb, w_a, w_b, zgate, zgate)


def _hgrn_head(q, k, b2, v16, st):
    half_row = lax.broadcasted_iota(jnp.int32, (SUBLANES, CHUNK), 0)
    half_lane = lax.broadcasted_iota(jnp.int32, (SUBLANES, CHUNK), 1)
    b_last = b2[CHUNK - 1:CHUNK]
    c2 = b2 - jnp.log2(k)

    o = lax.dot_general((q * jnp.exp2(b2)).astype(BF16), st.astype(BF16), _NT,
                        preferred_element_type=F32)
    a_rows = []
    for blk in range(CHUNK // SUB):
        lo = blk * SUB
        q_i, b_i, c_i = q[lo:lo + SUB], b2[lo:lo + SUB], c2[lo:lo + SUB]
        if blk == 0:
            a_top = jnp.zeros((SUBLANES, CHUNK), F32)
            a_bot = a_top
        else:
            beta = b2[lo - 1:lo]
            qt = (q_i * jnp.exp2(b_i - beta)).astype(BF16)
            kt = (k[:lo] * jnp.exp2(beta - b2[:lo])).astype(BF16)
            kt = jnp.concatenate([kt, jnp.zeros((CHUNK - lo, HEAD_DK), BF16)], axis=0)
            a_off = lax.dot_general(qt, kt, _NT, preferred_element_type=F32)
            a_top, a_bot = a_off[:SUBLANES], a_off[SUBLANES:]
        for jj in range(SUB):
            col = half_lane == lo + jj
            cj = c_i[jj:jj + 1]
            if jj < SUBLANES:
                s_top = jnp.sum(q_i[:SUBLANES] * jnp.exp2(b_i[:SUBLANES] - cj), axis=-1, keepdims=True)
                a_top = jnp.where(col, s_top, a_top)
            s_bot = jnp.sum(q_i[SUBLANES:] * jnp.exp2(b_i[SUBLANES:] - cj), axis=-1, keepdims=True)
            a_bot = jnp.where(col, s_bot, a_bot)
        a_top = jnp.where(half_lane > lo + half_row, 0.0, a_top)
        a_bot = jnp.where(half_lane > lo + SUBLANES + half_row, 0.0, a_bot)
        a_rows += [a_top, a_bot]
    a_mat = jnp.concatenate(a_rows, axis=0)
    o = o + jnp.dot(a_mat.astype(BF16), v16, preferred_element_type=F32)

    khat = (k * jnp.exp2(b_last - b2)).astype(BF16)
    st_new = st * jnp.exp2(b_last) + lax.dot_general(v16, khat, _TN, preferred_element_type=F32)
    return o, st_new


def _hgrn_kernel(lbl_ref, q_ref, f_ref, v_ref, g_ref, ng_ref, o_ref, st_ref, *, layer, blk_chunks, pad, group):
    t = pl.program_id(2)
    gw = group * HEAD_DK
    if layer > 0:
        lbl = lbl_ref[...]
        e = jnp.exp(lbl - jnp.max(lbl, axis=0, keepdims=True))
        p = e / jnp.sum(e, axis=0, keepdims=True)
        lb = p[1:2]
        for r in range(2, layer + 1):
            lb = lb + p[r:r + 1]
        log_lb = jnp.log(lb)
        log1m_lb = jnp.log1p(-lb)
    ng = ng_ref[...]

    row = lax.broadcasted_iota(jnp.int32, (CHUNK, 1), 0)
    tri_r = lax.broadcasted_iota(jnp.int32, (CHUNK, CHUNK), 0)
    tri_c = lax.broadcasted_iota(jnp.int32, (CHUNK, CHUNK), 1)
    tril = (tri_r >= tri_c).astype(F32)
    row0 = t * (blk_chunks * CHUNK)

    @pl.when(t == 0)
    def _():
        st_ref[...] = jnp.zeros_like(st_ref)

    def chunk(c):
        r0 = c * CHUNK if isinstance(c, int) else pl.multiple_of(c * CHUNK, CHUNK)
        valid = (row0 + r0 + row) >= pad
        qz = q_ref[pl.ds(r0, CHUNK), :].astype(F32)
        fz = f_ref[pl.ds(r0, CHUNK), :].astype(F32)
        v16 = v_ref[pl.ds(r0, CHUNK), :]
        gz = g_ref[pl.ds(r0, CHUNK), :].astype(F32)

        q = _silu(qz)
        ls = _log_sigmoid(fz)
        if layer == 0:
            log_f = ls
            k = jnp.exp(ls - fz)
        else:
            b_ = log1m_lb + ls
            log_f = jnp.maximum(log_lb, b_) + jnp.log1p(jnp.exp(-jnp.abs(log_lb - b_)))
            k = jnp.exp(b_ - fz)
        k = jnp.where(valid, k, 0.0)
        b2 = jnp.dot(tril, log_f * LOG2E, precision=_HI, preferred_element_type=F32)
        gate = _silu(gz) * ng

        for g in range(group):
            sl = slice(g * HEAD_DK, (g + 1) * HEAD_DK)
            o, st_new = _hgrn_head(q[:, sl], k[:, sl], b2[:, sl], v16[:, sl], st_ref[g])
            st_ref[g] = st_new
            y = o * lax.rsqrt(jnp.mean(o * o, axis=-1, keepdims=True) + NORM_EPS)
            o_ref[pl.ds(r0, CHUNK), sl] = (y * gate[:, sl]).astype(o_ref.dtype)

    _for_chunks(blk_chunks, chunk, HG_UNROLL)


def _hgrn(z, lb_logits, norm_g, layer, batch, l_pad, heads, pad):
    T = z.shape[0]
    depth = lb_logits.shape[0]
    group = _group(heads, HG_GROUP)
    ngrp = heads // group
    gw = group * HEAD_DK
    n_chunks = l_pad // CHUNK
    blk_chunks = _block_chunks(n_chunks)
    nt = n_chunks // blk_chunks
    tb = blk_chunks * CHUNK
    kern = functools.partial(_hgrn_kernel, layer=layer, blk_chunks=blk_chunks, pad=pad, group=group)
    seq_blk = lambda sec: pl.BlockSpec((tb, gw), lambda b, h, t: (b * nt + t, sec * ngrp + h))
    return pl.pallas_call(
        kern,
        grid=(batch, ngrp, nt),
        in_specs=[pl.BlockSpec((depth, gw), lambda b, h, t: (0, h)),
                  seq_blk(0), seq_blk(1), seq_blk(2), seq_blk(3),
                  pl.BlockSpec((1, gw), lambda b, h, t: (0, h))],
        out_specs=pl.BlockSpec((tb, gw), lambda b, h, t: (b * nt + t, h)),
        out_shape=jax.ShapeDtypeStruct((T, heads * HG_DV), BF16),
        scratch_shapes=[pltpu.VMEM((group, HG_DV, HEAD_DK), F32)],
        compiler_params=_params(("parallel", "parallel", "arbitrary")),
        name="hgrn2",
    )(lb_logits.astype(F32), z, z, z, z, norm_g.reshape(1, -1).astype(F32))


def _mlstm_head(q, k, v16, i_fin, fcum, valid, valid_col, cs, ns, m_prev):
    tri_r = lax.broadcasted_iota(jnp.int32, (CHUNK, CHUNK), 0)
    tri_c = lax.broadcasted_iota(jnp.int32, (CHUNK, CHUNK), 1)
    causal = tri_r >= tri_c
    i_pre = jnp.where(valid, i_fin, -jnp.inf)
    g_row = jnp.sum(jnp.where(tri_r == tri_c, fcum - i_fin, 0.0), axis=0, keepdims=True)
    log_d = jnp.where(jnp.logical_and(causal, valid_col), fcum - g_row, -jnp.inf)

    log_prev = fcum + m_prev
    m_t = jnp.maximum(log_prev, jnp.max(log_d, axis=-1, keepdims=True))
    w_prev = jnp.exp(log_prev - m_t)
    q16 = q.astype(BF16)
    sqk = lax.dot_general(q16, k.astype(BF16), _NT, preferred_element_type=F32) * jnp.exp(log_d - m_t)

    num = w_prev * jnp.dot(q16, cs.astype(BF16), preferred_element_type=F32) \
        + jnp.dot(sqk.astype(BF16), v16, preferred_element_type=F32)
    den = w_prev * jnp.sum(q * ns, axis=-1, keepdims=True) + jnp.sum(sqk, axis=-1, keepdims=True)
    hval = num / jnp.maximum(jnp.abs(den), jnp.exp(-m_t))

    m_new = m_t[CHUNK - 1:CHUNK]
    f_last = fcum[CHUNK - 1:CHUNK]
    w_old = jnp.exp(f_last + m_prev - m_new)
    kw = k * jnp.exp(f_last - fcum + i_pre - m_new)
    cs_new = w_old * cs + _tdot(kw.astype(BF16), v16)
    ns_new = w_old * ns + jnp.sum(kw, axis=0, keepdims=True)
    return hval, cs_new, ns_new, m_new


def _mlstm_kernel(zg_ref, gbias_ref, q_ref, k_ref, v_ref, og_ref, cwq_ref, cwk_ref, cbq_ref, cbk_ref,
                  ng_ref, o_ref, cs_ref, ns_ref, m_ref, pq_ref, pk_ref, *, blk_chunks, pad, heads, group):
    hg = pl.program_id(1)
    t = pl.program_id(2)
    row = lax.broadcasted_iota(jnp.int32, (CHUNK, 1), 0)
    lane = lax.broadcasted_iota(jnp.int32, (CHUNK, LANES), 1)
    tri_c = lax.broadcasted_iota(jnp.int32, (CHUNK, CHUNK), 1)
    tril = (lax.broadcasted_iota(jnp.int32, (CHUNK, CHUNK), 0) >= tri_c).astype(F32)
    gbias = gbias_ref[...]
    cwq, cwk = cwq_ref[...], cwk_ref[...]
    cbq, cbk = cbq_ref[...], cbk_ref[...]
    ng = ng_ref[...]
    k_scale = HEAD_DK ** -0.5
    row0 = t * (blk_chunks * CHUNK)

    @pl.when(t == 0)
    def _():
        cs_ref[...] = jnp.zeros_like(cs_ref)
        ns_ref[...] = jnp.zeros_like(ns_ref)
        m_ref[...] = jnp.zeros_like(m_ref)
        pq_ref[...] = jnp.zeros_like(pq_ref)
        pk_ref[...] = jnp.zeros_like(pk_ref)

    def conv_silu(cur, prev, w, bias):
        y = cur * w[CONV_W - 1:CONV_W] + bias
        for s in range(1, CONV_W):
            mixed = jnp.where(row >= CHUNK - s, prev, cur)
            y = y + pltpu.roll(mixed, s, 0) * w[CONV_W - 1 - s:CONV_W - s]
        return _silu(y)

    def chunk(c):
        r0 = c * CHUNK if isinstance(c, int) else pl.multiple_of(c * CHUNK, CHUNK)
        valid = (row0 + r0 + row) >= pad
        valid_col = (row0 + r0 + tri_c) >= pad

        q_in = jnp.where(valid, q_ref[pl.ds(r0, CHUNK), :].astype(F32), 0.0)
        k_in = jnp.where(valid, k_ref[pl.ds(r0, CHUNK), :].astype(F32), 0.0)
        q = conv_silu(q_in, pq_ref[...], cwq, cbq)
        k = conv_silu(k_in, pk_ref[...], cwk, cbk) * k_scale
        pq_ref[...] = q_in
        pk_ref[...] = k_in
        v16 = v_ref[pl.ds(r0, CHUNK), :]
        og = _sigmoid(og_ref[pl.ds(r0, CHUNK), :].astype(F32)) * ng
        capped = GATE_CAP * jnp.tanh((zg_ref[pl.ds(r0, CHUNK), :] + gbias) / GATE_CAP)
        log_f = jnp.where(valid, _log_sigmoid(capped), 0.0)
        fcum_all = jnp.dot(tril, log_f, precision=_HI, preferred_element_type=F32)

        for g in range(group):
            head = hg * group + g
            i_fin = jnp.sum(jnp.where(lane == head, capped, 0.0), axis=-1, keepdims=True)
            fcum = jnp.sum(jnp.where(lane == heads + head, fcum_all, 0.0), axis=-1, keepdims=True)
            sk = slice(g * HEAD_DK, (g + 1) * HEAD_DK)
            sv = slice(g * ML_DV, (g + 1) * ML_DV)
            hval, cs_new, ns_new, m_new = _mlstm_head(
                q[:, sk], k[:, sk], v16[:, sv], i_fin, fcum, valid, valid_col,
                cs_ref[g], ns_ref[g], m_ref[g])
            cs_ref[g] = cs_new
            ns_ref[g] = ns_new
            m_ref[g] = m_new
            y = hval * lax.rsqrt(jnp.mean(hval * hval, axis=-1, keepdims=True) + NORM_EPS)
            o_ref[pl.ds(r0, CHUNK), sv] = (y * og[:, sv]).astype(o_ref.dtype)

    _for_chunks(blk_chunks, chunk, ML_UNROLL)


def _mlstm(zq, zg, gate_bias, conv_w, conv_b, norm_g, batch, l_pad, heads, q_col, pad):
    T = zq.shape[0]
    group = _group(heads, ML_GROUP)
    ngrp = heads // group
    gk, gv = group * HEAD_DK, group * ML_DV
    qk_w = heads * HEAD_DK
    qb = q_col // gk
    kb = qb + ngrp
    vb = (q_col + 2 * qk_w) // gv
    ob = vb + ngrp
    assert q_col % gk == 0 and (q_col + 2 * qk_w) % gv == 0
    n_chunks = l_pad // CHUNK
    blk_chunks = _block_chunks(n_chunks)
    nt = n_chunks // blk_chunks
    tb = blk_chunks * CHUNK
    kern = functools.partial(_mlstm_kernel, blk_chunks=blk_chunks, pad=pad, heads=heads, group=group)
    rows = lambda b, h, t: b * nt + t
    return pl.pallas_call(
        kern,
        grid=(batch, ngrp, nt),
        in_specs=[pl.BlockSpec((tb, LANES), lambda b, h, t: (rows(b, h, t), 0)),
                  pl.BlockSpec((1, LANES), lambda b, h, t: (0, 0)),
                  pl.BlockSpec((tb, gk), lambda b, h, t: (rows(b, h, t), qb + h)),
                  pl.BlockSpec((tb, gk), lambda b, h, t: (rows(b, h, t), kb + h)),
                  pl.BlockSpec((tb, gv), lambda b, h, t: (rows(b, h, t), vb + h)),
                  pl.BlockSpec((tb, gv), lambda b, h, t: (rows(b, h, t), ob + h)),
                  pl.BlockSpec((CONV_W, gk), lambda b, h, t: (0, h)),
                  pl.BlockSpec((CONV_W, gk), lambda b, h, t: (0, ngrp + h)),
                  pl.BlockSpec((1, gk), lambda b, h, t: (0, h)),
                  pl.BlockSpec((1, gk), lambda b, h, t: (0, ngrp + h)),
                  pl.BlockSpec((1, gv), lambda b, h, t: (0, h))],
        out_specs=pl.BlockSpec((tb, gv), lambda b, h, t: (rows(b, h, t), h)),
        out_shape=jax.ShapeDtypeStruct((T, heads * ML_DV), BF16),
        scratch_shapes=[pltpu.VMEM((group, HEAD_DK, ML_DV), F32),
                        pltpu.VMEM((group, 1, HEAD_DK), F32),
                        pltpu.VMEM((group, 1, 1), F32),
                        pltpu.VMEM((CHUNK, gk), F32),
                        pltpu.VMEM((CHUNK, gk), F32)],
        compiler_params=_params(("parallel", "parallel", "arbitrary")),
        name="mlstm",
    )(zg, gate_bias, zq, zq, zq, zq, conv_w.astype(F32), conv_w.astype(F32),
      conv_b.reshape(1, -1).astype(F32), conv_b.reshape(1, -1).astype(F32),
      norm_g.reshape(1, -1).astype(F32))


def kernel(x, meta_tokens, hgrn_lb_logits, norm_ffn1, ffn1_w_gate, ffn1_w_up, ffn1_w_down, norm_mix, w_in, mlstm_conv_w, mlstm_conv_b, mlstm_igate_b, mlstm_fgate_b, hgrn_out_norm, mlstm_out_norm, w_branch_a, w_branch_b, w_out, norm_ffn2, ffn2_w_gate, ffn2_w_up, ffn2_w_down, final_norm):
    batch, seq, d_model = x.shape
    depth = w_in.shape[0]
    n_meta = meta_tokens.shape[0]
    hg_heads = w_branch_a.shape[1] // HG_DV
    ml_heads = mlstm_igate_b.shape[1]
    hg_w = hg_heads * HEAD_DK
    ml_qk = ml_heads * HEAD_DK
    ml_v = ml_heads * ML_DV
    n_gate = 2 * ml_heads
    assert n_gate <= LANES

    pad = (-(n_meta + seq)) % CHUNK
    l_pad = pad + n_meta + seq
    meta = jnp.broadcast_to(meta_tokens[None].astype(x.dtype), (batch, n_meta, d_model))
    h = jnp.concatenate([jnp.zeros((batch, pad, d_model), x.dtype), meta, x], axis=1)
    h = h.reshape(batch * l_pad, d_model)

    gates_col = 4 * hg_w + 2 * ml_qk + 2 * ml_v
    mlstm_q_col = 4 * hg_w
    assert gates_col + LANES <= w_in.shape[2]
    w_in_t = jnp.swapaxes(w_in, 1, 2)
    gate_bias = jnp.pad(jnp.concatenate([mlstm_igate_b, mlstm_fgate_b], axis=1).astype(F32),
                        ((0, 0), (0, LANES - n_gate)))
    w_g1, w_u1, w_d1 = ffn1_w_gate.astype(BF16), ffn1_w_up.astype(BF16), ffn1_w_down.astype(BF16)
    w_g2, w_u2, w_d2 = ffn2_w_gate.astype(BF16), ffn2_w_up.astype(BF16), ffn2_w_down.astype(BF16)
    w_a, w_b, w_o = w_branch_a.astype(BF16), w_branch_b.astype(BF16), w_out.astype(BF16)

    u, ss = _prep(h, norm_ffn1[0])
    for l in range(depth):
        a = _ffn_up(u, ss, w_g1, w_u1, l)
        h, u, ss = _resid_mm(a, w_d1, l, h, 0.5, norm_mix[l])

        zmix = _in_proj(u, ss, w_in_t, l, 0, gates_col, BF16)
        zg = _in_proj(u, ss, w_in_t, l, gates_col, LANES, F32)
        zmrg = _in_proj(u, ss, w_in_t, l, gates_col + n_gate, 2 * d_model, BF16)
        ya = _hgrn(zmix, hgrn_lb_logits, hgrn_out_norm[l], l, batch, l_pad, hg_heads, pad)
        yb = _mlstm(zmix, zg, gate_bias[l:l + 1], mlstm_conv_w[l], mlstm_conv_b[l], mlstm_out_norm[l],
                    batch, l_pad, ml_heads, mlstm_q_col, pad)
        y = _merge(ya, yb, w_a, w_b, l, zmrg)
        h, u, ss = _resid_mm(y, w_o, l, h, 1.0, norm_ffn2[l])

        a = _ffn_up(u, ss, w_g2, w_u2, l)
        next_gain = norm_ffn1[l + 1] if l + 1 < depth else final_norm
        h, u, ss = _resid_mm(a, w_d2, l, h, 0.5, next_gain)

    return _final_norm(h, final_norm, batch, l_pad, seq)
```

```python
import functools
import math

import jax
import jax.numpy as jnp
from jax import lax
from jax.experimental import pallas as pl
from jax.experimental.pallas import tpu as pltpu

F32 = jnp.float32
BF16 = jnp.bfloat16

CHUNK = 64
NORM_EPS = 1e-6
HEAD_DK = 128
HG_DV = 128
ML_DV = 256
CONV_W = 4
GATE_CAP = 15.0
SUB = 16
SUBLANES = 8
LANES = 128
VMEM_LIMIT = 56 * 1024 * 1024
HG_GROUP = 4
ML_GROUP = 4
HG_UNROLL = 4
ML_UNROLL = 3
MAX_BLOCK_CHUNKS = 48
LOG2E = 1.4426950408889634

_HI = lax.Precision.HIGHEST
_NT = (((1,), (1,)), ((), ()))
_TN = (((0,), (0,)), ((), ()))


def _params(sem):
    return pltpu.CompilerParams(dimension_semantics=sem, vmem_limit_bytes=VMEM_LIMIT)


def _row_tile(n_rows, target):
    return max(t for t in range(16, min(n_rows, target) + 1, 16) if n_rows % t == 0)


def _col_tile(n_cols, target):
    return max(t for t in range(LANES, min(n_cols, target) + 1, LANES) if n_cols % t == 0)


def _block_chunks(n_chunks):
    return max(d for d in range(1, min(n_chunks, MAX_BLOCK_CHUNKS) + 1) if n_chunks % d == 0)


def _group(heads, target):
    return max(g for g in range(1, target + 1) if heads % g == 0)


def _for_chunks(n, body, unroll):
    trips = n // unroll

    def trip(i, carry):
        for u in range(unroll):
            body(i * unroll + u)
        return carry

    if trips > 0:
        lax.fori_loop(0, trips, trip, 0)
    for c in range(trips * unroll, n):
        body(c)


def _tdot(a, b):
    n = a.shape[1]
    eye = (lax.broadcasted_iota(jnp.int32, (n, n), 0) ==
           lax.broadcasted_iota(jnp.int32, (n, n), 1)).astype(a.dtype)
    a_t = lax.dot_general(eye, a, _NT, preferred_element_type=F32).astype(a.dtype)
    return jnp.dot(a_t, b, preferred_element_type=F32)


def _log_sigmoid(z):
    return jnp.minimum(z, 0.0) - jnp.log(1.0 + jnp.exp(-jnp.abs(z)))


def _sigmoid(z):
    return 0.5 * jnp.tanh(0.5 * z) + 0.5


def _silu(z):
    return z * _sigmoid(z)


def _row_scale(ss_ref, width):
    return lax.rsqrt(ss_ref[:, :1] * (1.0 / width) + NORM_EPS)


def _prep_kernel(h_ref, g_ref, up_ref, ss_ref):
    x = h_ref[...]
    up_ref[...] = (x * g_ref[...]).astype(up_ref.dtype)
    ss_ref[...] = jnp.broadcast_to(jnp.sum(x * x, axis=-1, keepdims=True), ss_ref.shape)


def _prep(h, gain):
    T, D = h.shape
    tm = _row_tile(T, 384)
    return pl.pallas_call(
        _prep_kernel,
        grid=(T // tm,),
        in_specs=[pl.BlockSpec((tm, D), lambda i: (i, 0)),
                  pl.BlockSpec((1, D), lambda i: (0, 0))],
        out_specs=[pl.BlockSpec((tm, D), lambda i: (i, 0)),
                   pl.BlockSpec((tm, LANES), lambda i: (i, 0))],
        out_shape=[jax.ShapeDtypeStruct((T, D), BF16),
                   jax.ShapeDtypeStruct((T, LANES), F32)],
        compiler_params=_params(("parallel",)),
        name="prep",
    )(h, gain.reshape(1, D).astype(F32))


def _final_norm_kernel(h_ref, g_ref, o_ref):
    x = h_ref[...]
    y = x * lax.rsqrt(jnp.mean(x * x, axis=-1, keepdims=True) + NORM_EPS)
    o_ref[...] = y * g_ref[...]


def _final_norm(h, gain, batch, l_pad, seq):
    T, D = h.shape
    lead = l_pad - seq
    tm = _row_tile(seq, 512)
    assert lead % SUBLANES == 0
    return pl.pallas_call(
        _final_norm_kernel,
        grid=(batch, seq // tm),
        in_specs=[pl.BlockSpec((pl.Element(tm), pl.Element(D)),
                               lambda b, i: (pl.multiple_of(b * l_pad + lead + i * tm, SUBLANES), 0)),
                  pl.BlockSpec((1, D), lambda b, i: (0, 0))],
        out_specs=pl.BlockSpec((None, tm, D), lambda b, i: (b, i, 0)),
        out_shape=jax.ShapeDtypeStruct((batch, seq, D), F32),
        compiler_params=_params(("parallel", "parallel")),
        name="final_norm",
    )(h, gain.reshape(1, D).astype(F32))


def _ffn_up_kernel(u_ref, ss_ref, wg_ref, wu_ref, a_ref):
    u = u_ref[...]
    rs = _row_scale(ss_ref, u.shape[1])
    g = jnp.dot(u, wg_ref[...], preferred_element_type=F32) * rs
    up = jnp.dot(u, wu_ref[...], preferred_element_type=F32) * rs
    a_ref[...] = (_silu(g) * up).astype(a_ref.dtype)


def _ffn_up(u, ss, w_gate, w_up, layer):
    T, D = u.shape
    F = w_gate.shape[2]
    tm, tn = _row_tile(T, 1376), _col_tile(F, 512)
    return pl.pallas_call(
        _ffn_up_kernel,
        grid=(T // tm, F // tn),
        in_specs=[pl.BlockSpec((tm, D), lambda i, j: (i, 0)),
                  pl.BlockSpec((tm, LANES), lambda i, j: (i, 0)),
                  pl.BlockSpec((None, D, tn), lambda i, j: (layer, 0, j)),
                  pl.BlockSpec((None, D, tn), lambda i, j: (layer, 0, j))],
        out_specs=pl.BlockSpec((tm, tn), lambda i, j: (i, j)),
        out_shape=jax.ShapeDtypeStruct((T, F), BF16),
        compiler_params=_params(("parallel", "arbitrary")),
        name="ffn_up",
    )(u, ss, w_gate, w_up)


def _resid_mm_kernel(a_ref, w_ref, h_ref, g_ref, o_ref, up_ref, ss_ref, *, scale):
    @pl.when(pl.program_id(1) == 0)
    def _():
        ss_ref[...] = jnp.zeros_like(ss_ref)

    hn = h_ref[...] + scale * jnp.dot(a_ref[...], w_ref[...], preferred_element_type=F32)
    o_ref[...] = hn
    up_ref[...] = (hn * g_ref[...]).astype(up_ref.dtype)
    ss_ref[...] += jnp.broadcast_to(jnp.sum(hn * hn, axis=-1, keepdims=True), ss_ref.shape)


def _resid_mm(a, w, layer, h, scale, next_gain):
    T, K = a.shape
    N = w.shape[2]
    tm, tn = _row_tile(T, 1376), _col_tile(N, 512)
    return pl.pallas_call(
        functools.partial(_resid_mm_kernel, scale=scale),
        grid=(T // tm, N // tn),
        in_specs=[pl.BlockSpec((tm, K), lambda i, j: (i, 0)),
                  pl.BlockSpec((None, K, tn), lambda i, j: (layer, 0, j)),
                  pl.BlockSpec((tm, tn), lambda i, j: (i, j)),
                  pl.BlockSpec((1, tn), lambda i, j: (0, j))],
        out_specs=[pl.BlockSpec((tm, tn), lambda i, j: (i, j)),
                   pl.BlockSpec((tm, tn), lambda i, j: (i, j)),
                   pl.BlockSpec((tm, LANES), lambda i, j: (i, 0))],
        out_shape=[jax.ShapeDtypeStruct((T, N), F32),
                   jax.ShapeDtypeStruct((T, N), BF16),
                   jax.ShapeDtypeStruct((T, LANES), F32)],
        input_output_aliases={2: 0},
        compiler_params=_params(("parallel", "arbitrary")),
        name="resid_mm",
    )(a, w, h, next_gain.reshape(1, N).astype(F32))


def _in_proj_kernel(u_ref, ss_ref, wt_ref, o_ref):
    u = u_ref[...]
    acc = lax.dot_general(u, wt_ref[...].astype(u.dtype), _NT, preferred_element_type=F32)
    o_ref[...] = (acc * _row_scale(ss_ref, u.shape[1])).astype(o_ref.dtype)


def _in_proj(u, ss, w_t, layer, row0, n_cols, out_dtype):
    T, K = u.shape
    tn = _col_tile(n_cols, 512)
    tm = _row_tile(T, 1376 if n_cols > tn else 688)
    align = math.gcd(row0, tn)
    return pl.pallas_call(
        _in_proj_kernel,
        grid=(T // tm, n_cols // tn),
        in_specs=[pl.BlockSpec((tm, K), lambda i, j: (i, 0)),
                  pl.BlockSpec((tm, LANES), lambda i, j: (i, 0)),
                  pl.BlockSpec((None, pl.Element(tn), pl.Element(K)),
                               lambda i, j: (layer, pl.multiple_of(row0 + j * tn, align), 0))],
        out_specs=pl.BlockSpec((tm, tn), lambda i, j: (i, j)),
        out_shape=jax.ShapeDtypeStruct((T, n_cols), out_dtype),
        compiler_params=_params(("parallel", "arbitrary")),
        name="in_proj",
    )(u, ss, w_t)


def _merge_kernel(ya_ref, yb_ref, wa_ref, wb_ref, ga_ref, gb_ref, y_ref):
    pa = jnp.dot(ya_ref[...], wa_ref[...], preferred_element_type=F32)
    pb = jnp.dot(yb_ref[...], wb_ref[...], preferred_element_type=F32)
    y = _sigmoid(ga_ref[...].astype(F32)) * pa + _sigmoid(gb_ref[...].astype(F32)) * pb
    y_ref[...] = y.astype(y_ref.dtype)


def _merge(ya, yb, w_a, w_b, layer, zgate):
    T, Ka = ya.shape
    Kb = yb.shape[1]
    N = w_a.shape[2]
    tm, tn = _row_tile(T, 1376), _col_tile(N, 512)
    nb = N // tn
    return pl.pallas_call(
        _merge_kernel,
        grid=(T // tm, nb),
        in_specs=[pl.BlockSpec((tm, Ka), lambda i, j: (i, 0)),
                  pl.BlockSpec((tm, Kb), lambda i, j: (i, 0)),
                  pl.BlockSpec((None, Ka, tn), lambda i, j: (layer, 0, j)),
                  pl.BlockSpec((None, Kb, tn), lambda i, j: (layer, 0, j)),
                  pl.BlockSpec((tm, tn), lambda i, j: (i, j)),
                  pl.BlockSpec((tm, tn), lambda i, j: (i, nb + j))],
        out_specs=pl.BlockSpec((tm, tn), lambda i, j: (i, j)),
        out_shape=jax.ShapeDtypeStruct((T, N), BF16),
        compiler_params=_params(("parallel", "arbitrary")),
        name="merge",
    )(ya, yb, w_a, w_b, zgate, zgate)


def _hgrn_head(q, k, b2, v16, st):
    half_row = lax.broadcasted_iota(jnp.int32, (SUBLANES, CHUNK), 0)
    half_lane = lax.broadcasted_iota(jnp.int32, (SUBLANES, CHUNK), 1)
    b_last = b2[CHUNK - 1:CHUNK]
    c2 = b2 - jnp.log2(k)

    o = lax.dot_general((q * jnp.exp2(b2)).astype(BF16), st.astype(BF16), _NT,
                        preferred_element_type=F32)
    a_rows = []
    for blk in range(CHUNK // SUB):
        lo = blk * SUB
        q_i, b_i, c_i = q[lo:lo + SUB], b2[lo:lo + SUB], c2[lo:lo + SUB]
        if blk == 0:
            a_top = jnp.zeros((SUBLANES, CHUNK), F32)
            a_bot = a_top
        else:
            beta = b2[lo - 1:lo]
            qt = (q_i * jnp.exp2(b_i - beta)).astype(BF16)
            kt = (k[:lo] * jnp.exp2(beta - b2[:lo])).astype(BF16)
            kt = jnp.concatenate([kt, jnp.zeros((CHUNK - lo, HEAD_DK), BF16)], axis=0)
            a_off = lax.dot_general(qt, kt, _NT, preferred_element_type=F32)
            a_top, a_bot = a_off[:SUBLANES], a_off[SUBLANES:]
        for jj in range(SUB):
            col = half_lane == lo + jj
            cj = c_i[jj:jj + 1]
            if jj < SUBLANES:
                s_top = jnp.sum(q_i[:SUBLANES] * jnp.exp2(b_i[:SUBLANES] - cj), axis=-1, keepdims=True)
                a_top = jnp.where(col, s_top, a_top)
            s_bot = jnp.sum(q_i[SUBLANES:] * jnp.exp2(b_i[SUBLANES:] - cj), axis=-1, keepdims=True)
            a_bot = jnp.where(col, s_bot, a_bot)
        a_top = jnp.where(half_lane > lo + half_row, 0.0, a_top)
        a_bot = jnp.where(half_lane > lo + SUBLANES + half_row, 0.0, a_bot)
        a_rows += [a_top, a_bot]
    a_mat = jnp.concatenate(a_rows, axis=0)
    o = o + jnp.dot(a_mat.astype(BF16), v16, preferred_element_type=F32)

    khat = (k * jnp.exp2(b_last - b2)).astype(BF16)
    st_new = st * jnp.exp2(b_last) + lax.dot_general(v16, khat, _TN, preferred_element_type=F32)
    return o, st_new


def _hgrn_kernel(lbl_ref, q_ref, f_ref, v_ref, g_ref, ng_ref, o_ref, st_ref, *, layer, blk_chunks, pad, group):
    t = pl.program_id(2)
    gw = group * HEAD_DK
    if layer > 0:
        lbl = lbl_ref[...]
        e = jnp.exp(lbl - jnp.max(lbl, axis=0, keepdims=True))
        p = e / jnp.sum(e, axis=0, keepdims=True)
        lb = p[1:2]
        for r in range(2, layer + 1):
            lb = lb + p[r:r + 1]
        log_lb = jnp.log(lb)
        log1m_lb = jnp.log1p(-lb)
    ng = ng_ref[...]

    row = lax.broadcasted_iota(jnp.int32, (CHUNK, 1), 0)
    tri_r = lax.broadcasted_iota(jnp.int32, (CHUNK, CHUNK), 0)
    tri_c = lax.broadcasted_iota(jnp.int32, (CHUNK, CHUNK), 1)
    tril = (tri_r >= tri_c).astype(F32)
    row0 = t * (blk_chunks * CHUNK)

    @pl.when(t == 0)
    def _():
        st_ref[...] = jnp.zeros_like(st_ref)

    def chunk(c):
        r0 = c * CHUNK if isinstance(c, int) else pl.multiple_of(c * CHUNK, CHUNK)
        valid = (row0 + r0 + row) >= pad
        qz = q_ref[pl.ds(r0, CHUNK), :].astype(F32)
        fz = f_ref[pl.ds(r0, CHUNK), :].astype(F32)
        v16 = v_ref[pl.ds(r0, CHUNK), :]
        gz = g_ref[pl.ds(r0, CHUNK), :].astype(F32)

        q = _silu(qz)
        ls = _log_sigmoid(fz)
        if layer == 0:
            log_f = ls
            k = jnp.exp(ls - fz)
        else:
            b_ = log1m_lb + ls
            log_f = jnp.maximum(log_lb, b_) + jnp.log(1.0 + jnp.exp(-jnp.abs(log_lb - b_)))
            k = jnp.exp(b_ - fz)
        k = jnp.where(valid, k, 0.0)
        b2 = jnp.dot(tril, log_f * LOG2E, precision=_HI, preferred_element_type=F32)
        gate = _silu(gz) * ng

        for g in range(group):
            sl = slice(g * HEAD_DK, (g + 1) * HEAD_DK)
            o, st_new = _hgrn_head(q[:, sl], k[:, sl], b2[:, sl], v16[:, sl], st_ref[g])
            st_ref[g] = st_new
            y = o * lax.rsqrt(jnp.mean(o * o, axis=-1, keepdims=True) + NORM_EPS)
            o_ref[pl.ds(r0, CHUNK), sl] = (y * gate[:, sl]).astype(o_ref.dtype)

    _for_chunks(blk_chunks, chunk, HG_UNROLL)


def _hgrn(z, lb_logits, norm_g, layer, batch, l_pad, heads, pad):
    T = z.shape[0]
    depth = lb_logits.shape[0]
    group = _group(heads, HG_GROUP)
    ngrp = heads // group
    gw = group * HEAD_DK
    n_chunks = l_pad // CHUNK
    blk_chunks = _block_chunks(n_chunks)
    nt = n_chunks // blk_chunks
    tb = blk_chunks * CHUNK
    kern = functools.partial(_hgrn_kernel, layer=layer, blk_chunks=blk_chunks, pad=pad, group=group)
    seq_blk = lambda sec: pl.BlockSpec((tb, gw), lambda b, h, t: (b * nt + t, sec * ngrp + h))
    return pl.pallas_call(
        kern,
        grid=(batch, ngrp, nt),
        in_specs=[pl.BlockSpec((depth, gw), lambda b, h, t: (0, h)),
                  seq_blk(0), seq_blk(1), seq_blk(2), seq_blk(3),
                  pl.BlockSpec((1, gw), lambda b, h, t: (0, h))],
        out_specs=pl.BlockSpec((tb, gw), lambda b, h, t: (b * nt + t, h)),
        out_shape=jax.ShapeDtypeStruct((T, heads * HG_DV), BF16),
        scratch_shapes=[pltpu.VMEM((group, HG_DV, HEAD_DK), F32)],
        compiler_params=_params(("parallel", "parallel", "arbitrary")),
        name="hgrn2",
    )(lb_logits.astype(F32), z, z, z, z, norm_g.reshape(1, -1).astype(F32))


def _mlstm_head(q, k, v16, i_fin, fcum, valid, valid_col, cs, ns, m_prev):
    tri_r = lax.broadcasted_iota(jnp.int32, (CHUNK, CHUNK), 0)
    tri_c = lax.broadcasted_iota(jnp.int32, (CHUNK, CHUNK), 1)
    causal = tri_r >= tri_c
    i_pre = jnp.where(valid, i_fin, -jnp.inf)
    g_row = jnp.sum(jnp.where(tri_r == tri_c, fcum - i_fin, 0.0), axis=0, keepdims=True)
    log_d = jnp.where(jnp.logical_and(causal, valid_col), fcum - g_row, -jnp.inf)

    log_prev = fcum + m_prev
    m_t = jnp.maximum(log_prev, jnp.max(log_d, axis=-1, keepdims=True))
    w_prev = jnp.exp(log_prev - m_t)
    q16 = q.astype(BF16)
    sqk = lax.dot_general(q16, k.astype(BF16), _NT, preferred_element_type=F32) * jnp.exp(log_d - m_t)

    num = w_prev * jnp.dot(q16, cs.astype(BF16), preferred_element_type=F32) \
        + jnp.dot(sqk.astype(BF16), v16, preferred_element_type=F32)
    den = w_prev * jnp.sum(q * ns, axis=-1, keepdims=True) + jnp.sum(sqk, axis=-1, keepdims=True)
    hval = num / jnp.maximum(jnp.abs(den), jnp.exp(-m_t))

    m_new = m_t[CHUNK - 1:CHUNK]
    f_last = fcum[CHUNK - 1:CHUNK]
    w_old = jnp.exp(f_last + m_prev - m_new)
    kw = k * jnp.exp(f_last - fcum + i_pre - m_new)
    cs_new = w_old * cs + _tdot(kw.astype(BF16), v16)
    ns_new = w_old * ns + jnp.sum(kw, axis=0, keepdims=True)
    return hval, cs_new, ns_new, m_new


def _mlstm_kernel(zg_ref, gbias_ref, q_ref, k_ref, v_ref, og_ref, cwq_ref, cwk_ref, cbq_ref, cbk_ref,
                  ng_ref, o_ref, cs_ref, ns_ref, m_ref, pq_ref, pk_ref, *, blk_chunks, pad, heads, group):
    hg = pl.program_id(1)
    t = pl.program_id(2)
    row = lax.broadcasted_iota(jnp.int32, (CHUNK, 1), 0)
    lane = lax.broadcasted_iota(jnp.int32, (CHUNK, LANES), 1)
    tri_c = lax.broadcasted_iota(jnp.int32, (CHUNK, CHUNK), 1)
    tril = (lax.broadcasted_iota(jnp.int32, (CHUNK, CHUNK), 0) >= tri_c).astype(F32)
    gbias = gbias_ref[...]
    cwq, cwk = cwq_ref[...], cwk_ref[...]
    cbq, cbk = cbq_ref[...], cbk_ref[...]
    ng = ng_ref[...]
    k_scale = HEAD_DK ** -0.5
    row0 = t * (blk_chunks * CHUNK)

    @pl.when(t == 0)
    def _():
        cs_ref[...] = jnp.zeros_like(cs_ref)
        ns_ref[...] = jnp.zeros_like(ns_ref)
        m_ref[...] = jnp.zeros_like(m_ref)
        pq_ref[...] = jnp.zeros_like(pq_ref)
        pk_ref[...] = jnp.zeros_like(pk_ref)

    def conv_silu(cur, prev, w, bias):
        y = cur * w[CONV_W - 1:CONV_W] + bias
        for s in range(1, CONV_W):
            mixed = jnp.where(row >= CHUNK - s, prev, cur)
            y = y + pltpu.roll(mixed, s, 0) * w[CONV_W - 1 - s:CONV_W - s]
        return _silu(y)

    def chunk(c):
        r0 = c * CHUNK if isinstance(c, int) else pl.multiple_of(c * CHUNK, CHUNK)
        valid = (row0 + r0 + row) >= pad
        valid_col = (row0 + r0 + tri_c) >= pad

        q_in = jnp.where(valid, q_ref[pl.ds(r0, CHUNK), :].astype(F32), 0.0)
        k_in = jnp.where(valid, k_ref[pl.ds(r0, CHUNK), :].astype(F32), 0.0)
        q = conv_silu(q_in, pq_ref[...], cwq, cbq)
        k = conv_silu(k_in, pk_ref[...], cwk, cbk) * k_scale
        pq_ref[...] = q_in
        pk_ref[...] = k_in
        v16 = v_ref[pl.ds(r0, CHUNK), :]
        og = _sigmoid(og_ref[pl.ds(r0, CHUNK), :].astype(F32)) * ng
        capped = GATE_CAP * jnp.tanh((zg_ref[pl.ds(r0, CHUNK), :] + gbias) / GATE_CAP)
        log_f = jnp.where(valid, _log_sigmoid(capped), 0.0)
        fcum_all = jnp.dot(tril, log_f, precision=_HI, preferred_element_type=F32)

        for g in range(group):
            head = hg * group + g
            i_fin = jnp.sum(jnp.where(lane == head, capped, 0.0), axis=-1, keepdims=True)
            fcum = jnp.sum(jnp.where(lane == heads + head, fcum_all, 0.0), axis=-1, keepdims=True)
            sk = slice(g * HEAD_DK, (g + 1) * HEAD_DK)
            sv = slice(g * ML_DV, (g + 1) * ML_DV)
            hval, cs_new, ns_new, m_new = _mlstm_head(
                q[:, sk], k[:, sk], v16[:, sv], i_fin, fcum, valid, valid_col,
                cs_ref[g], ns_ref[g], m_ref[g])
            cs_ref[g] = cs_new
            ns_ref[g] = ns_new
            m_ref[g] = m_new
            y = hval * lax.rsqrt(jnp.mean(hval * hval, axis=-1, keepdims=True) + NORM_EPS)
            o_ref[pl.ds(r0, CHUNK), sv] = (y * og[:, sv]).astype(o_ref.dtype)

    _for_chunks(blk_chunks, chunk, ML_UNROLL)


def _mlstm(zq, zg, gate_bias, conv_w, conv_b, norm_g, batch, l_pad, heads, q_col, pad):
    T = zq.shape[0]
    group = _group(heads, ML_GROUP)
    ngrp = heads // group
    gk, gv = group * HEAD_DK, group * ML_DV
    qk_w = heads * HEAD_DK
    qb = q_col // gk
    kb = qb + ngrp
    vb = (q_col + 2 * qk_w) // gv
    ob = vb + ngrp
    assert q_col % gk == 0 and (q_col + 2 * qk_w) % gv == 0
    n_chunks = l_pad // CHUNK
    blk_chunks = _block_chunks(n_chunks)
    nt = n_chunks // blk_chunks
    tb = blk_chunks * CHUNK
    kern = functools.partial(_mlstm_kernel, blk_chunks=blk_chunks, pad=pad, heads=heads, group=group)
    rows = lambda b, h, t: b * nt + t
    return pl.pallas_call(
        kern,
        grid=(batch, ngrp, nt),
        in_specs=[pl.BlockSpec((tb, LANES), lambda b, h, t: (rows(b, h, t), 0)),
                  pl.BlockSpec((1, LANES), lambda b, h, t: (0, 0)),
                  pl.BlockSpec((tb, gk), lambda b, h, t: (rows(b, h, t), qb + h)),
                  pl.BlockSpec((tb, gk), lambda b, h, t: (rows(b, h, t), kb + h)),
                  pl.BlockSpec((tb, gv), lambda b, h, t: (rows(b, h, t), vb + h)),
                  pl.BlockSpec((tb, gv), lambda b, h, t: (rows(b, h, t), ob + h)),
                  pl.BlockSpec((CONV_W, gk), lambda b, h, t: (0, h)),
                  pl.BlockSpec((CONV_W, gk), lambda b, h, t: (0, ngrp + h)),
                  pl.BlockSpec((1, gk), lambda b, h, t: (0, h)),
                  pl.BlockSpec((1, gk), lambda b, h, t: (0, ngrp + h)),
                  pl.BlockSpec((1, gv), lambda b, h, t: (0, h))],
        out_specs=pl.BlockSpec((tb, gv), lambda b, h, t: (rows(b, h, t), h)),
        out_shape=jax.ShapeDtypeStruct((T, heads * ML_DV), BF16),
        scratch_shapes=[pltpu.VMEM((group, HEAD_DK, ML_DV), F32),
                        pltpu.VMEM((group, 1, HEAD_DK), F32),
                        pltpu.VMEM((group, 1, 1), F32),
                        pltpu.VMEM((CHUNK, gk), F32),
                        pltpu.VMEM((CHUNK, gk), F32)],
        compiler_params=_params(("parallel", "parallel", "arbitrary")),
        name="mlstm",
    )(zg, gate_bias, zq, zq, zq, zq, conv_w.astype(F32), conv_w.astype(F32),
      conv_b.reshape(1, -1).astype(F32), conv_b.reshape(1, -1).astype(F32),
      norm_g.reshape(1, -1).astype(F32))


def kernel(x, meta_tokens, hgrn_lb_logits, norm_ffn1, ffn1_w_gate, ffn1_w_up, ffn1_w_down, norm_mix, w_in, mlstm_conv_w, mlstm_conv_b, mlstm_igate_b, mlstm_fgate_b, hgrn_out_norm, mlstm_out_norm, w_branch_a, w_branch_b, w_out, norm_ffn2, ffn2_w_gate, ffn2_w_up, ffn2_w_down, final_norm):
    batch, seq, d_model = x.shape
    depth = w_in.shape[0]
    n_meta = meta_tokens.shape[0]
    hg_heads = w_branch_a.shape[1] // HG_DV
    ml_heads = mlstm_igate_b.shape[1]
    hg_w = hg_heads * HEAD_DK
    ml_qk = ml_heads * HEAD_DK
    ml_v = ml_heads * ML_DV
    n_gate = 2 * ml_heads
    assert n_gate <= LANES

    pad = (-(n_meta + seq)) % CHUNK
    l_pad = pad + n_meta + seq
    meta = jnp.broadcast_to(meta_tokens[None].astype(x.dtype), (batch, n_meta, d_model))
    h = jnp.concatenate([jnp.zeros((batch, pad, d_model), x.dtype), meta, x], axis=1)
    h = h.reshape(batch * l_pad, d_model)

    gates_col = 4 * hg_w + 2 * ml_qk + 2 * ml_v
    mlstm_q_col = 4 * hg_w
    assert gates_col + LANES <= w_in.shape[2]
    w_in_t = jnp.swapaxes(w_in, 1, 2)
    gate_bias = jnp.pad(jnp.concatenate([mlstm_igate_b, mlstm_fgate_b], axis=1).astype(F32),
                        ((0, 0), (0, LANES - n_gate)))
    w_g1, w_u1, w_d1 = ffn1_w_gate.astype(BF16), ffn1_w_up.astype(BF16), ffn1_w_down.astype(BF16)
    w_g2, w_u2, w_d2 = ffn2_w_gate.astype(BF16), ffn2_w_up.astype(BF16), ffn2_w_down.astype(BF16)
    w_a, w_b, w_o = w_branch_a.astype(BF16), w_branch_b.astype(BF16), w_out.astype(BF16)

    u, ss = _prep(h, norm_ffn1[0])
    for l in range(depth):
        a = _ffn_up(u, ss, w_g1, w_u1, l)
        h, u, ss = _resid_mm(a, w_d1, l, h, 0.5, norm_mix[l])

        zmix = _in_proj(u, ss, w_in_t, l, 0, gates_col, BF16)
        zg = _in_proj(u, ss, w_in_t, l, gates_col, LANES, F32)
        zmrg = _in_proj(u, ss, w_in_t, l, gates_col + n_gate, 2 * d_model, BF16)
        ya = _hgrn(zmix, hgrn_lb_logits, hgrn_out_norm[l], l, batch, l_pad, hg_heads, pad)
        yb = _mlstm(zmix, zg, gate_bias[l:l + 1], mlstm_conv_w[l], mlstm_conv_b[l], mlstm_out_norm[l],
                    batch, l_pad, ml_heads, mlstm_q_col, pad)
        y = _merge(ya, yb, w_a, w_b, l, zmrg)
        h, u, ss = _resid_mm(y, w_o, l, h, 1.0, norm_ffn2[l])

        a = _ffn_up(u, ss, w_g2, w_u2, l)
        next_gain = norm_ffn1[l + 1] if l + 1 < depth else final_norm
        h, u, ss = _resid_mm(a, w_d2, l, h, 0.5, next_gain)

    return _final_norm(h, final_norm, batch, l_pad, seq)
```

```python
import functools
import math

import jax
import jax.numpy as jnp
from jax import lax
from jax.experimental import pallas as pl
from jax.experimental.pallas import tpu as pltpu

F32 = jnp.float32
BF16 = jnp.bfloat16

CHUNK = 64
NORM_EPS = 1e-6
HEAD_DK = 128
HG_DV = 128
ML_DV = 256
CONV_W = 4
GATE_CAP = 15.0
SUB = 16
SUBLANES = 8
LANES = 128
VMEM_LIMIT = 56 * 1024 * 1024
HG_GROUP = 4
ML_GROUP = 4
HG_UNROLL = 4
ML_UNROLL = 3
MAX_BLOCK_CHUNKS = 48
LOG2E = 1.4426950408889634

_HI = lax.Precision.HIGHEST
_NT = (((1,), (1,)), ((), ()))
_TN = (((0,), (0,)), ((), ()))


def _params(sem):
    return pltpu.CompilerParams(dimension_semantics=sem, vmem_limit_bytes=VMEM_LIMIT)


def _row_tile(n_rows, target):
    return max(t for t in range(16, min(n_rows, target) + 1, 16) if n_rows % t == 0)


def _col_tile(n_cols, target):
    return max(t for t in range(LANES, min(n_cols, target) + 1, LANES) if n_cols % t == 0)


def _block_chunks(n_chunks):
    return max(d for d in range(1, min(n_chunks, MAX_BLOCK_CHUNKS) + 1) if n_chunks % d == 0)


def _group(heads, target):
    return max(g for g in range(1, target + 1) if heads % g == 0)


def _for_chunks(n, body, unroll):
    trips = n // unroll

    def trip(i, carry):
        for u in range(unroll):
            body(i * unroll + u)
        return carry

    if trips > 0:
        lax.fori_loop(0, trips, trip, 0)
    for c in range(trips * unroll, n):
        body(c)


def _tdot(a, b):
    n = a.shape[1]
    eye = (lax.broadcasted_iota(jnp.int32, (n, n), 0) ==
           lax.broadcasted_iota(jnp.int32, (n, n), 1)).astype(a.dtype)
    a_t = lax.dot_general(eye, a, _NT, preferred_element_type=F32).astype(a.dtype)
    return jnp.dot(a_t, b, preferred_element_type=F32)


def _log_sigmoid(z):
    return jnp.minimum(z, 0.0) - jnp.log(1.0 + jnp.exp(-jnp.abs(z)))


def _sigmoid(z):
    return 0.5 * jnp.tanh(0.5 * z) + 0.5


def _silu(z):
    return z * _sigmoid(z)


def _row_scale(ss_ref, width):
    return lax.rsqrt(ss_ref[:, :1] * (1.0 / width) + NORM_EPS)


def _prep_kernel(h_ref, g_ref, up_ref, ss_ref):
    x = h_ref[...]
    up_ref[...] = (x * g_ref[...]).astype(up_ref.dtype)
    ss_ref[...] = jnp.broadcast_to(jnp.sum(x * x, axis=-1, keepdims=True), ss_ref.shape)


def _prep(h, gain):
    T, D = h.shape
    tm = _row_tile(T, 384)
    return pl.pallas_call(
        _prep_kernel,
        grid=(T // tm,),
        in_specs=[pl.BlockSpec((tm, D), lambda i: (i, 0)),
                  pl.BlockSpec((1, D), lambda i: (0, 0))],
        out_specs=[pl.BlockSpec((tm, D), lambda i: (i, 0)),
                   pl.BlockSpec((tm, LANES), lambda i: (i, 0))],
        out_shape=[jax.ShapeDtypeStruct((T, D), BF16),
                   jax.ShapeDtypeStruct((T, LANES), F32)],
        compiler_params=_params(("parallel",)),
        name="prep",
    )(h, gain.reshape(1, D).astype(F32))


def _final_norm_kernel(h_ref, g_ref, o_ref):
    x = h_ref[...]
    y = x * lax.rsqrt(jnp.mean(x * x, axis=-1, keepdims=True) + NORM_EPS)
    o_ref[...] = y * g_ref[...]


def _final_norm(h, gain, batch, l_pad, seq):
    T, D = h.shape
    lead = l_pad - seq
    tm = _row_tile(seq, 512)
    assert lead % SUBLANES == 0
    return pl.pallas_call(
        _final_norm_kernel,
        grid=(batch, seq // tm),
        in_specs=[pl.BlockSpec((pl.Element(tm), pl.Element(D)),
                               lambda b, i: (pl.multiple_of(b * l_pad + lead + i * tm, SUBLANES), 0)),
                  pl.BlockSpec((1, D), lambda b, i: (0, 0))],
        out_specs=pl.BlockSpec((None, tm, D), lambda b, i: (b, i, 0)),
        out_shape=jax.ShapeDtypeStruct((batch, seq, D), F32),
        compiler_params=_params(("parallel", "parallel")),
        name="final_norm",
    )(h, gain.reshape(1, D).astype(F32))


def _ffn_up_kernel(u_ref, ss_ref, wg_ref, wu_ref, a_ref):
    u = u_ref[...]
    rs = _row_scale(ss_ref, u.shape[1])
    g = jnp.dot(u, wg_ref[...], preferred_element_type=F32) * rs
    up = jnp.dot(u, wu_ref[...], preferred_element_type=F32) * rs
    a_ref[...] = (_silu(g) * up).astype(a_ref.dtype)


def _ffn_up(u, ss, w_gate, w_up, layer):
    T, D = u.shape
    F = w_gate.shape[2]
    tm, tn = _row_tile(T, 1376), _col_tile(F, 512)
    return pl.pallas_call(
        _ffn_up_kernel,
        grid=(T // tm, F // tn),
        in_specs=[pl.BlockSpec((tm, D), lambda i, j: (i, 0)),
                  pl.BlockSpec((tm, LANES), lambda i, j: (i, 0)),
                  pl.BlockSpec((None, D, tn), lambda i, j: (layer, 0, j)),
                  pl.BlockSpec((None, D, tn), lambda i, j: (layer, 0, j))],
        out_specs=pl.BlockSpec((tm, tn), lambda i, j: (i, j)),
        out_shape=jax.ShapeDtypeStruct((T, F), BF16),
        compiler_params=_params(("parallel", "arbitrary")),
        name="ffn_up",
    )(u, ss, w_gate, w_up)


def _resid_mm_kernel(a_ref, w_ref, h_ref, g_ref, o_ref, up_ref, ss_ref, *, scale):
    @pl.when(pl.program_id(1) == 0)
    def _():
        ss_ref[...] = jnp.zeros_like(ss_ref)

    hn = h_ref[...] + scale * jnp.dot(a_ref[...], w_ref[...], preferred_element_type=F32)
    o_ref[...] = hn
    up_ref[...] = (hn * g_ref[...]).astype(up_ref.dtype)
    ss_ref[...] += jnp.broadcast_to(jnp.sum(hn * hn, axis=-1, keepdims=True), ss_ref.shape)


def _resid_mm(a, w, layer, h, scale, next_gain):
    T, K = a.shape
    N = w.shape[2]
    tm, tn = _row_tile(T, 1376), _col_tile(N, 512)
    return pl.pallas_call(
        functools.partial(_resid_mm_kernel, scale=scale),
        grid=(T // tm, N // tn),
        in_specs=[pl.BlockSpec((tm, K), lambda i, j: (i, 0)),
                  pl.BlockSpec((None, K, tn), lambda i, j: (layer, 0, j)),
                  pl.BlockSpec((tm, tn), lambda i, j: (i, j)),
                  pl.BlockSpec((1, tn), lambda i, j: (0, j))],
        out_specs=[pl.BlockSpec((tm, tn), lambda i, j: (i, j)),
                   pl.BlockSpec((tm, tn), lambda i, j: (i, j)),
                   pl.BlockSpec((tm, LANES), lambda i, j: (i, 0))],
        out_shape=[jax.ShapeDtypeStruct((T, N), F32),
                   jax.ShapeDtypeStruct((T, N), BF16),
                   jax.ShapeDtypeStruct((T, LANES), F32)],
        input_output_aliases={2: 0},
        compiler_params=_params(("parallel", "arbitrary")),
        name="resid_mm",
    )(a, w, h, next_gain.reshape(1, N).astype(F32))


def _in_proj_kernel(u_ref, ss_ref, wt_ref, o_ref):
    u = u_ref[...]
    acc = lax.dot_general(u, wt_ref[...].astype(u.dtype), _NT, preferred_element_type=F32)
    o_ref[...] = (acc * _row_scale(ss_ref, u.shape[1])).astype(o_ref.dtype)


def _in_proj(u, ss, w_t, layer, row0, n_cols, out_dtype):
    T, K = u.shape
    tn = _col_tile(n_cols, 512)
    tm = _row_tile(T, 1376 if n_cols > tn else 688)
    align = math.gcd(row0, tn)
    return pl.pallas_call(
        _in_proj_kernel,
        grid=(T // tm, n_cols // tn),
        in_specs=[pl.BlockSpec((tm, K), lambda i, j: (i, 0)),
                  pl.BlockSpec((tm, LANES), lambda i, j: (i, 0)),
                  pl.BlockSpec((None, pl.Element(tn), pl.Element(K)),
                               lambda i, j: (layer, pl.multiple_of(row0 + j * tn, align), 0))],
        out_specs=pl.BlockSpec((tm, tn), lambda i, j: (i, j)),
        out_shape=jax.ShapeDtypeStruct((T, n_cols), out_dtype),
        compiler_params=_params(("parallel", "arbitrary")),
        name="in_proj",
    )(u, ss, w_t)


def _merge_kernel(ya_ref, yb_ref, wa_ref, wb_ref, ga_ref, gb_ref, y_ref):
    pa = jnp.dot(ya_ref[...], wa_ref[...], preferred_element_type=F32)
    pb = jnp.dot(yb_ref[...], wb_ref[...], preferred_element_type=F32)
    y = _sigmoid(ga_ref[...].astype(F32)) * pa + _sigmoid(gb_ref[...].astype(F32)) * pb
    y_ref[...] = y.astype(y_ref.dtype)


def _merge(ya, yb, w_a, w_b, layer, zgate):
    T, Ka = ya.shape
    Kb = yb.shape[1]
    N = w_a.shape[2]
    tm, tn = _row_tile(T, 688), _col_tile(N, 1024)
    nb = N // tn
    return pl.pallas_call(
        _merge_kernel,
        grid=(T // tm, nb),
        in_specs=[pl.BlockSpec((tm, Ka), lambda i, j: (i, 0)),
                  pl.BlockSpec((tm, Kb), lambda i, j: (i, 0)),
                  pl.BlockSpec((None, Ka, tn), lambda i, j: (layer, 0, j)),
                  pl.BlockSpec((None, Kb, tn), lambda i, j: (layer, 0, j)),
                  pl.BlockSpec((tm, tn), lambda i, j: (i, j)),
                  pl.BlockSpec((tm, tn), lambda i, j: (i, nb + j))],
        out_specs=pl.BlockSpec((tm, tn), lambda i, j: (i, j)),
        out_shape=jax.ShapeDtypeStruct((T, N), BF16),
        compiler_params=_params(("parallel", "arbitrary")),
        name="merge",
    )(ya, yb, w_a, w_b, zgate, zgate)


def _hgrn_head(q, k, b2, v16, st):
    half_row = lax.broadcasted_iota(jnp.int32, (SUBLANES, CHUNK), 0)
    half_lane = lax.broadcasted_iota(jnp.int32, (SUBLANES, CHUNK), 1)
    b_last = b2[CHUNK - 1:CHUNK]
    c2 = b2 - jnp.log2(k)

    o = lax.dot_general((q * jnp.exp2(b2)).astype(BF16), st.astype(BF16), _NT,
                        preferred_element_type=F32)
    a_rows = []
    for blk in range(CHUNK // SUB):
        lo = blk * SUB
        q_i, b_i, c_i = q[lo:lo + SUB], b2[lo:lo + SUB], c2[lo:lo + SUB]
        if blk == 0:
            a_top = jnp.zeros((SUBLANES, CHUNK), F32)
            a_bot = a_top
        else:
            beta = b2[lo - 1:lo]
            qt = (q_i * jnp.exp2(b_i - beta)).astype(BF16)
            kt = (k[:lo] * jnp.exp2(beta - b2[:lo])).astype(BF16)
            kt = jnp.concatenate([kt, jnp.zeros((CHUNK - lo, HEAD_DK), BF16)], axis=0)
            a_off = lax.dot_general(qt, kt, _NT, preferred_element_type=F32)
            a_top, a_bot = a_off[:SUBLANES], a_off[SUBLANES:]
        for jj in range(SUB):
            col = half_lane == lo + jj
            cj = c_i[jj:jj + 1]
            if jj < SUBLANES:
                s_top = jnp.sum(q_i[:SUBLANES] * jnp.exp2(b_i[:SUBLANES] - cj), axis=-1, keepdims=True)
                a_top = jnp.where(col, s_top, a_top)
            s_bot = jnp.sum(q_i[SUBLANES:] * jnp.exp2(b_i[SUBLANES:] - cj), axis=-1, keepdims=True)
            a_bot = jnp.where(col, s_bot, a_bot)
        a_top = jnp.where(half_lane > lo + half_row, 0.0, a_top)
        a_bot = jnp.where(half_lane > lo + SUBLANES + half_row, 0.0, a_bot)
        a_rows += [a_top, a_bot]
    a_mat = jnp.concatenate(a_rows, axis=0)
    o = o + jnp.dot(a_mat.astype(BF16), v16, preferred_element_type=F32)

    khat = (k * jnp.exp2(b_last - b2)).astype(BF16)
    st_new = st * jnp.exp2(b_last) + lax.dot_general(v16, khat, _TN, preferred_element_type=F32)
    return o, st_new


def _hgrn_kernel(lbl_ref, q_ref, f_ref, v_ref, g_ref, ng_ref, o_ref, st_ref, *, layer, blk_chunks, pad, group):
    t = pl.program_id(2)
    gw = group * HEAD_DK
    if layer > 0:
        lbl = lbl_ref[...]
        e = jnp.exp(lbl - jnp.max(lbl, axis=0, keepdims=True))
        p = e / jnp.sum(e, axis=0, keepdims=True)
        lb = p[1:2]
        for r in range(2, layer + 1):
            lb = lb + p[r:r + 1]
        log_lb = jnp.log(lb)
        log1m_lb = jnp.log1p(-lb)
    ng = ng_ref[...]

    row = lax.broadcasted_iota(jnp.int32, (CHUNK, 1), 0)
    tri_r = lax.broadcasted_iota(jnp.int32, (CHUNK, CHUNK), 0)
    tri_c = lax.broadcasted_iota(jnp.int32, (CHUNK, CHUNK), 1)
    tril = (tri_r >= tri_c).astype(F32)
    row0 = t * (blk_chunks * CHUNK)

    @pl.when(t == 0)
    def _():
        st_ref[...] = jnp.zeros_like(st_ref)

    def chunk(c):
        r0 = c * CHUNK if isinstance(c, int) else pl.multiple_of(c * CHUNK, CHUNK)
        valid = (row0 + r0 + row) >= pad
        qz = q_ref[pl.ds(r0, CHUNK), :].astype(F32)
        fz = f_ref[pl.ds(r0, CHUNK), :].astype(F32)
        v16 = v_ref[pl.ds(r0, CHUNK), :]
        gz = g_ref[pl.ds(r0, CHUNK), :].astype(F32)

        q = _silu(qz)
        ls = _log_sigmoid(fz)
        if layer == 0:
            log_f = ls
            k = jnp.exp(ls - fz)
        else:
            b_ = log1m_lb + ls
            log_f = jnp.maximum(log_lb, b_) + jnp.log(1.0 + jnp.exp(-jnp.abs(log_lb - b_)))
            k = jnp.exp(b_ - fz)
        k = jnp.where(valid, k, 0.0)
        b2 = jnp.dot(tril, log_f * LOG2E, precision=_HI, preferred_element_type=F32)
        gate = _silu(gz) * ng

        for g in range(group):
            sl = slice(g * HEAD_DK, (g + 1) * HEAD_DK)
            o, st_new = _hgrn_head(q[:, sl], k[:, sl], b2[:, sl], v16[:, sl], st_ref[g])
            st_ref[g] = st_new
            y = o * lax.rsqrt(jnp.mean(o * o, axis=-1, keepdims=True) + NORM_EPS)
            o_ref[pl.ds(r0, CHUNK), sl] = (y * gate[:, sl]).astype(o_ref.dtype)

    _for_chunks(blk_chunks, chunk, HG_UNROLL)


def _hgrn(z, lb_logits, norm_g, layer, batch, l_pad, heads, pad):
    T = z.shape[0]
    depth = lb_logits.shape[0]
    group = _group(heads, HG_GROUP)
    ngrp = heads // group
    gw = group * HEAD_DK
    n_chunks = l_pad // CHUNK
    blk_chunks = _block_chunks(n_chunks)
    nt = n_chunks // blk_chunks
    tb = blk_chunks * CHUNK
    kern = functools.partial(_hgrn_kernel, layer=layer, blk_chunks=blk_chunks, pad=pad, group=group)
    seq_blk = lambda sec: pl.BlockSpec((tb, gw), lambda b, h, t: (b * nt + t, sec * ngrp + h))
    return pl.pallas_call(
        kern,
        grid=(batch, ngrp, nt),
        in_specs=[pl.BlockSpec((depth, gw), lambda b, h, t: (0, h)),
                  seq_blk(0), seq_blk(1), seq_blk(2), seq_blk(3),
                  pl.BlockSpec((1, gw), lambda b, h, t: (0, h))],
        out_specs=pl.BlockSpec((tb, gw), lambda b, h, t: (b * nt + t, h)),
        out_shape=jax.ShapeDtypeStruct((T, heads * HG_DV), BF16),
        scratch_shapes=[pltpu.VMEM((group, HG_DV, HEAD_DK), F32)],
        compiler_params=_params(("parallel", "parallel", "arbitrary")),
        name="hgrn2",
    )(lb_logits.astype(F32), z, z, z, z, norm_g.reshape(1, -1).astype(F32))


def _mlstm_head(q, k, v16, i_fin, fcum, valid, valid_col, cs, ns, m_prev):
    tri_r = lax.broadcasted_iota(jnp.int32, (CHUNK, CHUNK), 0)
    tri_c = lax.broadcasted_iota(jnp.int32, (CHUNK, CHUNK), 1)
    causal = tri_r >= tri_c
    i_pre = jnp.where(valid, i_fin, -jnp.inf)
    g_row = jnp.sum(jnp.where(tri_r == tri_c, fcum - i_fin, 0.0), axis=0, keepdims=True)
    log_d = jnp.where(jnp.logical_and(causal, valid_col), fcum - g_row, -jnp.inf)

    log_prev = fcum + m_prev
    m_t = jnp.maximum(log_prev, jnp.max(log_d, axis=-1, keepdims=True))
    w_prev = jnp.exp(log_prev - m_t)
    q16 = q.astype(BF16)
    sqk = lax.dot_general(q16, k.astype(BF16), _NT, preferred_element_type=F32) * jnp.exp(log_d - m_t)

    num = w_prev * jnp.dot(q16, cs.astype(BF16), preferred_element_type=F32) \
        + jnp.dot(sqk.astype(BF16), v16, preferred_element_type=F32)
    den = w_prev * jnp.sum(q * ns, axis=-1, keepdims=True) + jnp.sum(sqk, axis=-1, keepdims=True)
    hval = num / jnp.maximum(jnp.abs(den), jnp.exp(-m_t))

    m_new = m_t[CHUNK - 1:CHUNK]
    f_last = fcum[CHUNK - 1:CHUNK]
    w_old = jnp.exp(f_last + m_prev - m_new)
    kw = k * jnp.exp(f_last - fcum + i_pre - m_new)
    cs_new = w_old * cs + _tdot(kw.astype(BF16), v16)
    ns_new = w_old * ns + jnp.sum(kw, axis=0, keepdims=True)
    return hval, cs_new, ns_new, m_new


def _mlstm_kernel(zg_ref, gbias_ref, q_ref, k_ref, v_ref, og_ref, cwq_ref, cwk_ref, cbq_ref, cbk_ref,
                  ng_ref, o_ref, cs_ref, ns_ref, m_ref, pq_ref, pk_ref, *, blk_chunks, pad, heads, group):
    hg = pl.program_id(1)
    t = pl.program_id(2)
    row = lax.broadcasted_iota(jnp.int32, (CHUNK, 1), 0)
    lane = lax.broadcasted_iota(jnp.int32, (CHUNK, LANES), 1)
    tri_c = lax.broadcasted_iota(jnp.int32, (CHUNK, CHUNK), 1)
    tril = (lax.broadcasted_iota(jnp.int32, (CHUNK, CHUNK), 0) >= tri_c).astype(F32)
    gbias = gbias_ref[...]
    cwq, cwk = cwq_ref[...], cwk_ref[...]
    cbq, cbk = cbq_ref[...], cbk_ref[...]
    ng = ng_ref[...]
    k_scale = HEAD_DK ** -0.5
    row0 = t * (blk_chunks * CHUNK)

    @pl.when(t == 0)
    def _():
        cs_ref[...] = jnp.zeros_like(cs_ref)
        ns_ref[...] = jnp.zeros_like(ns_ref)
        m_ref[...] = jnp.zeros_like(m_ref)
        pq_ref[...] = jnp.zeros_like(pq_ref)
        pk_ref[...] = jnp.zeros_like(pk_ref)

    def conv_silu(cur, prev, w, bias):
        y = cur * w[CONV_W - 1:CONV_W] + bias
        for s in range(1, CONV_W):
            mixed = jnp.where(row >= CHUNK - s, prev, cur)
            y = y + pltpu.roll(mixed, s, 0) * w[CONV_W - 1 - s:CONV_W - s]
        return _silu(y)

    def chunk(c):
        r0 = c * CHUNK if isinstance(c, int) else pl.multiple_of(c * CHUNK, CHUNK)
        valid = (row0 + r0 + row) >= pad
        valid_col = (row0 + r0 + tri_c) >= pad

        q_in = jnp.where(valid, q_ref[pl.ds(r0, CHUNK), :].astype(F32), 0.0)
        k_in = jnp.where(valid, k_ref[pl.ds(r0, CHUNK), :].astype(F32), 0.0)
        q = conv_silu(q_in, pq_ref[...], cwq, cbq)
        k = conv_silu(k_in, pk_ref[...], cwk, cbk) * k_scale
        pq_ref[...] = q_in
        pk_ref[...] = k_in
        v16 = v_ref[pl.ds(r0, CHUNK), :]
        og = _sigmoid(og_ref[pl.ds(r0, CHUNK), :].astype(F32)) * ng
        capped = GATE_CAP * jnp.tanh((zg_ref[pl.ds(r0, CHUNK), :] + gbias) / GATE_CAP)
        log_f = jnp.where(valid, _log_sigmoid(capped), 0.0)
        fcum_all = jnp.dot(tril, log_f, precision=_HI, preferred_element_type=F32)

        for g in range(group):
            head = hg * group + g
            i_fin = jnp.sum(jnp.where(lane == head, capped, 0.0), axis=-1, keepdims=True)
            fcum = jnp.sum(jnp.where(lane == heads + head, fcum_all, 0.0), axis=-1, keepdims=True)
            sk = slice(g * HEAD_DK, (g + 1) * HEAD_DK)
            sv = slice(g * ML_DV, (g + 1) * ML_DV)
            hval, cs_new, ns_new, m_new = _mlstm_head(
                q[:, sk], k[:, sk], v16[:, sv], i_fin, fcum, valid, valid_col,
                cs_ref[g], ns_ref[g], m_ref[g])
            cs_ref[g] = cs_new
            ns_ref[g] = ns_new
            m_ref[g] = m_new
            y = hval * lax.rsqrt(jnp.mean(hval * hval, axis=-1, keepdims=True) + NORM_EPS)
            o_ref[pl.ds(r0, CHUNK), sv] = (y * og[:, sv]).astype(o_ref.dtype)

    _for_chunks(blk_chunks, chunk, ML_UNROLL)


def _mlstm(zq, zg, gate_bias, conv_w, conv_b, norm_g, batch, l_pad, heads, q_col, pad):
    T = zq.shape[0]
    group = _group(heads, ML_GROUP)
    ngrp = heads // group
    gk, gv = group * HEAD_DK, group * ML_DV
    qk_w = heads * HEAD_DK
    qb = q_col // gk
    kb = qb + ngrp
    vb = (q_col + 2 * qk_w) // gv
    ob = vb + ngrp
    assert q_col % gk == 0 and (q_col + 2 * qk_w) % gv == 0
    n_chunks = l_pad // CHUNK
    blk_chunks = _block_chunks(n_chunks)
    nt = n_chunks // blk_chunks
    tb = blk_chunks * CHUNK
    kern = functools.partial(_mlstm_kernel, blk_chunks=blk_chunks, pad=pad, heads=heads, group=group)
    rows = lambda b, h, t: b * nt + t
    return pl.pallas_call(
        kern,
        grid=(batch, ngrp, nt),
        in_specs=[pl.BlockSpec((tb, LANES), lambda b, h, t: (rows(b, h, t), 0)),
                  pl.BlockSpec((1, LANES), lambda b, h, t: (0, 0)),
                  pl.BlockSpec((tb, gk), lambda b, h, t: (rows(b, h, t), qb + h)),
                  pl.BlockSpec((tb, gk), lambda b, h, t: (rows(b, h, t), kb + h)),
                  pl.BlockSpec((tb, gv), lambda b, h, t: (rows(b, h, t), vb + h)),
                  pl.BlockSpec((tb, gv), lambda b, h, t: (rows(b, h, t), ob + h)),
                  pl.BlockSpec((CONV_W, gk), lambda b, h, t: (0, h)),
                  pl.BlockSpec((CONV_W, gk), lambda b, h, t: (0, ngrp + h)),
                  pl.BlockSpec((1, gk), lambda b, h, t: (0, h)),
                  pl.BlockSpec((1, gk), lambda b, h, t: (0, ngrp + h)),
                  pl.BlockSpec((1, gv), lambda b, h, t: (0, h))],
        out_specs=pl.BlockSpec((tb, gv), lambda b, h, t: (rows(b, h, t), h)),
        out_shape=jax.ShapeDtypeStruct((T, heads * ML_DV), BF16),
        scratch_shapes=[pltpu.VMEM((group, HEAD_DK, ML_DV), F32),
                        pltpu.VMEM((group, 1, HEAD_DK), F32),
                        pltpu.VMEM((group, 1, 1), F32),
                        pltpu.VMEM((CHUNK, gk), F32),
                        pltpu.VMEM((CHUNK, gk), F32)],
        compiler_params=_params(("parallel", "parallel", "arbitrary")),
        name="mlstm",
    )(zg, gate_bias, zq, zq, zq, zq, conv_w.astype(F32), conv_w.astype(F32),
      conv_b.reshape(1, -1).astype(F32), conv_b.reshape(1, -1).astype(F32),
      norm_g.reshape(1, -1).astype(F32))


def kernel(x, meta_tokens, hgrn_lb_logits, norm_ffn1, ffn1_w_gate, ffn1_w_up, ffn1_w_down, norm_mix, w_in, mlstm_conv_w, mlstm_conv_b, mlstm_igate_b, mlstm_fgate_b, hgrn_out_norm, mlstm_out_norm, w_branch_a, w_branch_b, w_out, norm_ffn2, ffn2_w_gate, ffn2_w_up, ffn2_w_down, final_norm):
    batch, seq, d_model = x.shape
    depth = w_in.shape[0]
    n_meta = meta_tokens.shape[0]
    hg_heads = w_branch_a.shape[1] // HG_DV
    ml_heads = mlstm_igate_b.shape[1]
    hg_w = hg_heads * HEAD_DK
    ml_qk = ml_heads * HEAD_DK
    ml_v = ml_heads * ML_DV
    n_gate = 2 * ml_heads
    assert n_gate <= LANES

    pad = (-(n_meta + seq)) % CHUNK
    l_pad = pad + n_meta + seq
    meta = jnp.broadcast_to(meta_tokens[None].astype(x.dtype), (batch, n_meta, d_model))
    h = jnp.concatenate([jnp.zeros((batch, pad, d_model), x.dtype), meta, x], axis=1)
    h = h.reshape(batch * l_pad, d_model)

    gates_col = 4 * hg_w + 2 * ml_qk + 2 * ml_v
    mlstm_q_col = 4 * hg_w
    assert gates_col + LANES <= w_in.shape[2]
    w_in_t = jnp.swapaxes(w_in, 1, 2)
    gate_bias = jnp.pad(jnp.concatenate([mlstm_igate_b, mlstm_fgate_b], axis=1).astype(F32),
                        ((0, 0), (0, LANES - n_gate)))
    w_g1, w_u1, w_d1 = ffn1_w_gate.astype(BF16), ffn1_w_up.astype(BF16), ffn1_w_down.astype(BF16)
    w_g2, w_u2, w_d2 = ffn2_w_gate.astype(BF16), ffn2_w_up.astype(BF16), ffn2_w_down.astype(BF16)
    w_a, w_b, w_o = w_branch_a.astype(BF16), w_branch_b.astype(BF16), w_out.astype(BF16)

    u, ss = _prep(h, norm_ffn1[0])
    for l in range(depth):
        a = _ffn_up(u, ss, w_g1, w_u1, l)
        h, u, ss = _resid_mm(a, w_d1, l, h, 0.5, norm_mix[l])

        zmix = _in_proj(u, ss, w_in_t, l, 0, gates_col, BF16)
        zg = _in_proj(u, ss, w_in_t, l, gates_col, LANES, F32)
        zmrg = _in_proj(u, ss, w_in_t, l, gates_col + n_gate, 2 * d_model, BF16)
        ya = _hgrn(zmix, hgrn_lb_logits, hgrn_out_norm[l], l, batch, l_pad, hg_heads, pad)
        yb = _mlstm(zmix, zg, gate_bias[l:l + 1], mlstm_conv_w[l], mlstm_conv_b[l], mlstm_out_norm[l],
                    batch, l_pad, ml_heads, mlstm_q_col, pad)
        y = _merge(ya, yb, w_a, w_b, l, zmrg)
        h, u, ss = _resid_mm(y, w_o, l, h, 1.0, norm_ffn2[l])

        a = _ffn_up(u, ss, w_g2, w_u2, l)
        next_gain = norm_ffn1[l + 1] if l + 1 < depth else final_norm
        h, u, ss = _resid_mm(a, w_d2, l, h, 0.5, next_gain)

    return _final_norm(h, final_norm, batch, l_pad, seq)
```

```python
import functools
import math

import jax
import jax.numpy as jnp
from jax import lax
from jax.experimental import pallas as pl
from jax.experimental.pallas import tpu as pltpu

F32 = jnp.float32
BF16 = jnp.bfloat16

CHUNK = 64
NORM_EPS = 1e-6
HEAD_DK = 128
HG_DV = 128
ML_DV = 256
CONV_W = 4
GATE_CAP = 15.0
SUB = 16
SUBLANES = 8
LANES = 128
VMEM_LIMIT = 56 * 1024 * 1024
HG_GROUP = 4
ML_GROUP = 4
HG_UNROLL = 4
ML_UNROLL = 3
MAX_BLOCK_CHUNKS = 48
LOG2E = 1.4426950408889634

_HI = lax.Precision.HIGHEST
_NT = (((1,), (1,)), ((), ()))
_TN = (((0,), (0,)), ((), ()))


def _params(sem):
    return pltpu.CompilerParams(dimension_semantics=sem, vmem_limit_bytes=VMEM_LIMIT)


def _row_tile(n_rows, target):
    return max(t for t in range(16, min(n_rows, target) + 1, 16) if n_rows % t == 0)


def _col_tile(n_cols, target):
    return max(t for t in range(LANES, min(n_cols, target) + 1, LANES) if n_cols % t == 0)


def _block_chunks(n_chunks):
    return max(d for d in range(1, min(n_chunks, MAX_BLOCK_CHUNKS) + 1) if n_chunks % d == 0)


def _group(heads, target):
    return max(g for g in range(1, target + 1) if heads % g == 0)


def _for_chunks(n, body, unroll):
    trips = n // unroll

    def trip(i, carry):
        for u in range(unroll):
            body(i * unroll + u)
        return carry

    if trips > 0:
        lax.fori_loop(0, trips, trip, 0)
    for c in range(trips * unroll, n):
        body(c)


def _tdot(a, b):
    n = a.shape[1]
    eye = (lax.broadcasted_iota(jnp.int32, (n, n), 0) ==
           lax.broadcasted_iota(jnp.int32, (n, n), 1)).astype(a.dtype)
    a_t = lax.dot_general(eye, a, _NT, preferred_element_type=F32).astype(a.dtype)
    return jnp.dot(a_t, b, preferred_element_type=F32)


def _log_sigmoid(z):
    return jnp.minimum(z, 0.0) - jnp.log(1.0 + jnp.exp(-jnp.abs(z)))


def _sigmoid(z):
    return 0.5 * jnp.tanh(0.5 * z) + 0.5


def _silu(z):
    return z * _sigmoid(z)


def _row_scale(ss_ref, width):
    return lax.rsqrt(ss_ref[:, :1] * (1.0 / width) + NORM_EPS)


def _prep_kernel(h_ref, g_ref, up_ref, ss_ref):
    x = h_ref[...]
    up_ref[...] = (x * g_ref[...]).astype(up_ref.dtype)
    ss_ref[...] = jnp.broadcast_to(jnp.sum(x * x, axis=-1, keepdims=True), ss_ref.shape)


def _prep(h, gain):
    T, D = h.shape
    tm = _row_tile(T, 384)
    return pl.pallas_call(
        _prep_kernel,
        grid=(T // tm,),
        in_specs=[pl.BlockSpec((tm, D), lambda i: (i, 0)),
                  pl.BlockSpec((1, D), lambda i: (0, 0))],
        out_specs=[pl.BlockSpec((tm, D), lambda i: (i, 0)),
                   pl.BlockSpec((tm, LANES), lambda i: (i, 0))],
        out_shape=[jax.ShapeDtypeStruct((T, D), BF16),
                   jax.ShapeDtypeStruct((T, LANES), F32)],
        compiler_params=_params(("parallel",)),
        name="prep",
    )(h, gain.reshape(1, D).astype(F32))


def _final_norm_kernel(h_ref, g_ref, o_ref):
    x = h_ref[...]
    y = x * lax.rsqrt(jnp.mean(x * x, axis=-1, keepdims=True) + NORM_EPS)
    o_ref[...] = y * g_ref[...]


def _final_norm(h, gain, batch, l_pad, seq):
    T, D = h.shape
    lead = l_pad - seq
    tm = _row_tile(seq, 512)
    assert lead % SUBLANES == 0
    return pl.pallas_call(
        _final_norm_kernel,
        grid=(batch, seq // tm),
        in_specs=[pl.BlockSpec((pl.Element(tm), pl.Element(D)),
                               lambda b, i: (pl.multiple_of(b * l_pad + lead + i * tm, SUBLANES), 0)),
                  pl.BlockSpec((1, D), lambda b, i: (0, 0))],
        out_specs=pl.BlockSpec((None, tm, D), lambda b, i: (b, i, 0)),
        out_shape=jax.ShapeDtypeStruct((batch, seq, D), F32),
        compiler_params=_params(("parallel", "parallel")),
        name="final_norm",
    )(h, gain.reshape(1, D).astype(F32))


def _ffn_up_kernel(u_ref, ss_ref, wg_ref, wu_ref, a_ref):
    u = u_ref[...]
    rs = _row_scale(ss_ref, u.shape[1])
    g = jnp.dot(u, wg_ref[...], preferred_element_type=F32) * rs
    up = jnp.dot(u, wu_ref[...], preferred_element_type=F32) * rs
    a_ref[...] = (_silu(g) * up).astype(a_ref.dtype)


def _ffn_up(u, ss, w_gate, w_up, layer):
    T, D = u.shape
    F = w_gate.shape[2]
    tm, tn = _row_tile(T, 1376), _col_tile(F, 512)
    return pl.pallas_call(
        _ffn_up_kernel,
        grid=(T // tm, F // tn),
        in_specs=[pl.BlockSpec((tm, D), lambda i, j: (i, 0)),
                  pl.BlockSpec((tm, LANES), lambda i, j: (i, 0)),
                  pl.BlockSpec((None, D, tn), lambda i, j: (layer, 0, j)),
                  pl.BlockSpec((None, D, tn), lambda i, j: (layer, 0, j))],
        out_specs=pl.BlockSpec((tm, tn), lambda i, j: (i, j)),
        out_shape=jax.ShapeDtypeStruct((T, F), BF16),
        compiler_params=_params(("parallel", "arbitrary")),
        name="ffn_up",
    )(u, ss, w_gate, w_up)


def _resid_mm_kernel(a_ref, w_ref, h_ref, g_ref, o_ref, up_ref, ss_ref, *, scale):
    @pl.when(pl.program_id(1) == 0)
    def _():
        ss_ref[...] = jnp.zeros_like(ss_ref)

    hn = h_ref[...] + scale * jnp.dot(a_ref[...], w_ref[...], preferred_element_type=F32)
    o_ref[...] = hn
    up_ref[...] = (hn * g_ref[...]).astype(up_ref.dtype)
    ss_ref[...] += jnp.broadcast_to(jnp.sum(hn * hn, axis=-1, keepdims=True), ss_ref.shape)


def _resid_mm(a, w, layer, h, scale, next_gain):
    T, K = a.shape
    N = w.shape[2]
    tm, tn = _row_tile(T, 688), _col_tile(N, 1024)
    return pl.pallas_call(
        functools.partial(_resid_mm_kernel, scale=scale),
        grid=(T // tm, N // tn),
        in_specs=[pl.BlockSpec((tm, K), lambda i, j: (i, 0)),
                  pl.BlockSpec((None, K, tn), lambda i, j: (layer, 0, j)),
                  pl.BlockSpec((tm, tn), lambda i, j: (i, j)),
                  pl.BlockSpec((1, tn), lambda i, j: (0, j))],
        out_specs=[pl.BlockSpec((tm, tn), lambda i, j: (i, j)),
                   pl.BlockSpec((tm, tn), lambda i, j: (i, j)),
                   pl.BlockSpec((tm, LANES), lambda i, j: (i, 0))],
        out_shape=[jax.ShapeDtypeStruct((T, N), F32),
                   jax.ShapeDtypeStruct((T, N), BF16),
                   jax.ShapeDtypeStruct((T, LANES), F32)],
        input_output_aliases={2: 0},
        compiler_params=_params(("parallel", "arbitrary")),
        name="resid_mm",
    )(a, w, h, next_gain.reshape(1, N).astype(F32))


def _in_proj_kernel(u_ref, ss_ref, wt_ref, o_ref):
    u = u_ref[...]
    acc = lax.dot_general(u, wt_ref[...].astype(u.dtype), _NT, preferred_element_type=F32)
    o_ref[...] = (acc * _row_scale(ss_ref, u.shape[1])).astype(o_ref.dtype)


def _in_proj(u, ss, w_t, layer, row0, n_cols, out_dtype):
    T, K = u.shape
    tn = _col_tile(n_cols, 512)
    tm = _row_tile(T, 1376 if n_cols > tn else 688)
    align = math.gcd(row0, tn)
    return pl.pallas_call(
        _in_proj_kernel,
        grid=(T // tm, n_cols // tn),
        in_specs=[pl.BlockSpec((tm, K), lambda i, j: (i, 0)),
                  pl.BlockSpec((tm, LANES), lambda i, j: (i, 0)),
                  pl.BlockSpec((None, pl.Element(tn), pl.Element(K)),
                               lambda i, j: (layer, pl.multiple_of(row0 + j * tn, align), 0))],
        out_specs=pl.BlockSpec((tm, tn), lambda i, j: (i, j)),
        out_shape=jax.ShapeDtypeStruct((T, n_cols), out_dtype),
        compiler_params=_params(("parallel", "arbitrary")),
        name="in_proj",
    )(u, ss, w_t)


def _merge_kernel(ya_ref, yb_ref, wa_ref, wb_ref, ga_ref, gb_ref, y_ref):
    pa = jnp.dot(ya_ref[...], wa_ref[...], preferred_element_type=F32)
    pb = jnp.dot(yb_ref[...], wb_ref[...], preferred_element_type=F32)
    y = _sigmoid(ga_ref[...].astype(F32)) * pa + _sigmoid(gb_ref[...].astype(F32)) * pb
    y_ref[...] = y.astype(y_ref.dtype)


def _merge(ya, yb, w_a, w_b, layer, zgate):
    T, Ka = ya.shape
    Kb = yb.shape[1]
    N = w_a.shape[2]
    tm, tn = _row_tile(T, 688), _col_tile(N, 1024)
    nb = N // tn
    return pl.pallas_call(
        _merge_kernel,
        grid=(T // tm, nb),
        in_specs=[pl.BlockSpec((tm, Ka), lambda i, j: (i, 0)),
                  pl.BlockSpec((tm, Kb), lambda i, j: (i, 0)),
                  pl.BlockSpec((None, Ka, tn), lambda i, j: (layer, 0, j)),
                  pl.BlockSpec((None, Kb, tn), lambda i, j: (layer, 0, j)),
                  pl.BlockSpec((tm, tn), lambda i, j: (i, j)),
                  pl.BlockSpec((tm, tn), lambda i, j: (i, nb + j))],
        out_specs=pl.BlockSpec((tm, tn), lambda i, j: (i, j)),
        out_shape=jax.ShapeDtypeStruct((T, N), BF16),
        compiler_params=_params(("parallel", "arbitrary")),
        name="merge",
    )(ya, yb, w_a, w_b, zgate, zgate)


def _hgrn_head(q, k, b2, v16, st):
    half_row = lax.broadcasted_iota(jnp.int32, (SUBLANES, CHUNK), 0)
    half_lane = lax.broadcasted_iota(jnp.int32, (SUBLANES, CHUNK), 1)
    b_last = b2[CHUNK - 1:CHUNK]
    c2 = b2 - jnp.log2(k)

    o = lax.dot_general((q * jnp.exp2(b2)).astype(BF16), st.astype(BF16), _NT,
                        preferred_element_type=F32)
    a_rows = []
    for blk in range(CHUNK // SUB):
        lo = blk * SUB
        q_i, b_i, c_i = q[lo:lo + SUB], b2[lo:lo + SUB], c2[lo:lo + SUB]
        if blk == 0:
            a_top = jnp.zeros((SUBLANES, CHUNK), F32)
            a_bot = a_top
        else:
            beta = b2[lo - 1:lo]
            qt = (q_i * jnp.exp2(b_i - beta)).astype(BF16)
            kt = (k[:lo] * jnp.exp2(beta - b2[:lo])).astype(BF16)
            kt = jnp.concatenate([kt, jnp.zeros((CHUNK - lo, HEAD_DK), BF16)], axis=0)
            a_off = lax.dot_general(qt, kt, _NT, preferred_element_type=F32)
            a_top, a_bot = a_off[:SUBLANES], a_off[SUBLANES:]
        for jj in range(SUB):
            col = half_lane == lo + jj
            cj = c_i[jj:jj + 1]
            if jj < SUBLANES:
                s_top = jnp.sum(q_i[:SUBLANES] * jnp.exp2(b_i[:SUBLANES] - cj), axis=-1, keepdims=True)
                a_top = jnp.where(col, s_top, a_top)
            s_bot = jnp.sum(q_i[SUBLANES:] * jnp.exp2(b_i[SUBLANES:] - cj), axis=-1, keepdims=True)
            a_bot = jnp.where(col, s_bot, a_bot)
        a_top = jnp.where(half_lane > lo + half_row, 0.0, a_top)
        a_bot = jnp.where(half_lane > lo + SUBLANES + half_row, 0.0, a_bot)
        a_rows += [a_top, a_bot]
    a_mat = jnp.concatenate(a_rows, axis=0)
    o = o + jnp.dot(a_mat.astype(BF16), v16, preferred_element_type=F32)

    khat = (k * jnp.exp2(b_last - b2)).astype(BF16)
    st_new = st * jnp.exp2(b_last) + lax.dot_general(v16, khat, _TN, preferred_element_type=F32)
    return o, st_new


def _hgrn_kernel(lbl_ref, q_ref, f_ref, v_ref, g_ref, ng_ref, o_ref, st_ref, *, layer, blk_chunks, pad, group):
    t = pl.program_id(2)
    gw = group * HEAD_DK
    if layer > 0:
        lbl = lbl_ref[...]
        e = jnp.exp(lbl - jnp.max(lbl, axis=0, keepdims=True))
        p = e / jnp.sum(e, axis=0, keepdims=True)
        lb = p[1:2]
        for r in range(2, layer + 1):
            lb = lb + p[r:r + 1]
        log_lb = jnp.log(lb)
        log1m_lb = jnp.log1p(-lb)
    ng = ng_ref[...]

    row = lax.broadcasted_iota(jnp.int32, (CHUNK, 1), 0)
    tri_r = lax.broadcasted_iota(jnp.int32, (CHUNK, CHUNK), 0)
    tri_c = lax.broadcasted_iota(jnp.int32, (CHUNK, CHUNK), 1)
    tril = (tri_r >= tri_c).astype(F32)
    row0 = t * (blk_chunks * CHUNK)

    @pl.when(t == 0)
    def _():
        st_ref[...] = jnp.zeros_like(st_ref)

    def chunk(c):
        r0 = c * CHUNK if isinstance(c, int) else pl.multiple_of(c * CHUNK, CHUNK)
        valid = (row0 + r0 + row) >= pad
        qz = q_ref[pl.ds(r0, CHUNK), :].astype(F32)
        fz = f_ref[pl.ds(r0, CHUNK), :].astype(F32)
        v16 = v_ref[pl.ds(r0, CHUNK), :]
        gz = g_ref[pl.ds(r0, CHUNK), :].astype(F32)

        q = _silu(qz)
        ls = _log_sigmoid(fz)
        if layer == 0:
            log_f = ls
            k = jnp.exp(ls - fz)
        else:
            b_ = log1m_lb + ls
            log_f = jnp.maximum(log_lb, b_) + jnp.log(1.0 + jnp.exp(-jnp.abs(log_lb - b_)))
            k = jnp.exp(b_ - fz)
        k = jnp.where(valid, k, 0.0)
        b2 = jnp.dot(tril, log_f * LOG2E, precision=_HI, preferred_element_type=F32)
        gate = _silu(gz) * ng

        for g in range(group):
            sl = slice(g * HEAD_DK, (g + 1) * HEAD_DK)
            o, st_new = _hgrn_head(q[:, sl], k[:, sl], b2[:, sl], v16[:, sl], st_ref[g])
            st_ref[g] = st_new
            y = o * lax.rsqrt(jnp.mean(o * o, axis=-1, keepdims=True) + NORM_EPS)
            o_ref[pl.ds(r0, CHUNK), sl] = (y * gate[:, sl]).astype(o_ref.dtype)

    _for_chunks(blk_chunks, chunk, HG_UNROLL)


def _hgrn(z, lb_logits, norm_g, layer, batch, l_pad, heads, pad):
    T = z.shape[0]
    depth = lb_logits.shape[0]
    group = _group(heads, HG_GROUP)
    ngrp = heads // group
    gw = group * HEAD_DK
    n_chunks = l_pad // CHUNK
    blk_chunks = _block_chunks(n_chunks)
    nt = n_chunks // blk_chunks
    tb = blk_chunks * CHUNK
    kern = functools.partial(_hgrn_kernel, layer=layer, blk_chunks=blk_chunks, pad=pad, group=group)
    seq_blk = lambda sec: pl.BlockSpec((tb, gw), lambda b, h, t: (b * nt + t, sec * ngrp + h))
    return pl.pallas_call(
        kern,
        grid=(batch, ngrp, nt),
        in_specs=[pl.BlockSpec((depth, gw), lambda b, h, t: (0, h)),
                  seq_blk(0), seq_blk(1), seq_blk(2), seq_blk(3),
                  pl.BlockSpec((1, gw), lambda b, h, t: (0, h))],
        out_specs=pl.BlockSpec((tb, gw), lambda b, h, t: (b * nt + t, h)),
        out_shape=jax.ShapeDtypeStruct((T, heads * HG_DV), BF16),
        scratch_shapes=[pltpu.VMEM((group, HG_DV, HEAD_DK), F32)],
        compiler_params=_params(("parallel", "parallel", "arbitrary")),
        name="hgrn2",
    )(lb_logits.astype(F32), z, z, z, z, norm_g.reshape(1, -1).astype(F32))


def _mlstm_head(q, k, v16, i_fin, fcum, valid, valid_col, cs, ns, m_prev):
    tri_r = lax.broadcasted_iota(jnp.int32, (CHUNK, CHUNK), 0)
    tri_c = lax.broadcasted_iota(jnp.int32, (CHUNK, CHUNK), 1)
    causal = tri_r >= tri_c
    i_pre = jnp.where(valid, i_fin, -jnp.inf)
    g_row = jnp.sum(jnp.where(tri_r == tri_c, fcum - i_fin, 0.0), axis=0, keepdims=True)
    log_d = jnp.where(jnp.logical_and(causal, valid_col), fcum - g_row, -jnp.inf)

    log_prev = fcum + m_prev
    m_t = jnp.maximum(log_prev, jnp.max(log_d, axis=-1, keepdims=True))
    w_prev = jnp.exp(log_prev - m_t)
    q16 = q.astype(BF16)
    sqk = lax.dot_general(q16, k.astype(BF16), _NT, preferred_element_type=F32) * jnp.exp(log_d - m_t)

    num = w_prev * jnp.dot(q16, cs.astype(BF16), preferred_element_type=F32) \
        + jnp.dot(sqk.astype(BF16), v16, preferred_element_type=F32)
    den = w_prev * jnp.sum(q * ns, axis=-1, keepdims=True) + jnp.sum(sqk, axis=-1, keepdims=True)
    hval = num / jnp.maximum(jnp.abs(den), jnp.exp(-m_t))

    m_new = m_t[CHUNK - 1:CHUNK]
    f_last = fcum[CHUNK - 1:CHUNK]
    w_old = jnp.exp(f_last + m_prev - m_new)
    kw = k * jnp.exp(f_last - fcum + i_pre - m_new)
    cs_new = w_old * cs + _tdot(kw.astype(BF16), v16)
    ns_new = w_old * ns + jnp.sum(kw, axis=0, keepdims=True)
    return hval, cs_new, ns_new, m_new


def _mlstm_kernel(zg_ref, gbias_ref, q_ref, k_ref, v_ref, og_ref, cwq_ref, cwk_ref, cbq_ref, cbk_ref,
                  ng_ref, o_ref, cs_ref, ns_ref, m_ref, pq_ref, pk_ref, *, blk_chunks, pad, heads, group):
    hg = pl.program_id(1)
    t = pl.program_id(2)
    row = lax.broadcasted_iota(jnp.int32, (CHUNK, 1), 0)
    lane = lax.broadcasted_iota(jnp.int32, (CHUNK, LANES), 1)
    tri_c = lax.broadcasted_iota(jnp.int32, (CHUNK, CHUNK), 1)
    tril = (lax.broadcasted_iota(jnp.int32, (CHUNK, CHUNK), 0) >= tri_c).astype(F32)
    gbias = gbias_ref[...]
    cwq, cwk = cwq_ref[...], cwk_ref[...]
    cbq, cbk = cbq_ref[...], cbk_ref[...]
    ng = ng_ref[...]
    k_scale = HEAD_DK ** -0.5
    row0 = t * (blk_chunks * CHUNK)

    @pl.when(t == 0)
    def _():
        cs_ref[...] = jnp.zeros_like(cs_ref)
        ns_ref[...] = jnp.zeros_like(ns_ref)
        m_ref[...] = jnp.zeros_like(m_ref)
        pq_ref[...] = jnp.zeros_like(pq_ref)
        pk_ref[...] = jnp.zeros_like(pk_ref)

    def conv_silu(cur, prev, w, bias):
        y = cur * w[CONV_W - 1:CONV_W] + bias
        for s in range(1, CONV_W):
            mixed = jnp.where(row >= CHUNK - s, prev, cur)
            y = y + pltpu.roll(mixed, s, 0) * w[CONV_W - 1 - s:CONV_W - s]
        return _silu(y)

    def chunk(c):
        r0 = c * CHUNK if isinstance(c, int) else pl.multiple_of(c * CHUNK, CHUNK)
        valid = (row0 + r0 + row) >= pad
        valid_col = (row0 + r0 + tri_c) >= pad

        q_in = jnp.where(valid, q_ref[pl.ds(r0, CHUNK), :].astype(F32), 0.0)
        k_in = jnp.where(valid, k_ref[pl.ds(r0, CHUNK), :].astype(F32), 0.0)
        q = conv_silu(q_in, pq_ref[...], cwq, cbq)
        k = conv_silu(k_in, pk_ref[...], cwk, cbk) * k_scale
        pq_ref[...] = q_in
        pk_ref[...] = k_in
        v16 = v_ref[pl.ds(r0, CHUNK), :]
        og = _sigmoid(og_ref[pl.ds(r0, CHUNK), :].astype(F32)) * ng
        capped = GATE_CAP * jnp.tanh((zg_ref[pl.ds(r0, CHUNK), :] + gbias) / GATE_CAP)
        log_f = jnp.where(valid, _log_sigmoid(capped), 0.0)
        fcum_all = jnp.dot(tril, log_f, precision=_HI, preferred_element_type=F32)

        for g in range(group):
            head = hg * group + g
            i_fin = jnp.sum(jnp.where(lane == head, capped, 0.0), axis=-1, keepdims=True)
            fcum = jnp.sum(jnp.where(lane == heads + head, fcum_all, 0.0), axis=-1, keepdims=True)
            sk = slice(g * HEAD_DK, (g + 1) * HEAD_DK)
            sv = slice(g * ML_DV, (g + 1) * ML_DV)
            hval, cs_new, ns_new, m_new = _mlstm_head(
                q[:, sk], k[:, sk], v16[:, sv], i_fin, fcum, valid, valid_col,
                cs_ref[g], ns_ref[g], m_ref[g])
            cs_ref[g] = cs_new
            ns_ref[g] = ns_new
            m_ref[g] = m_new
            y = hval * lax.rsqrt(jnp.mean(hval * hval, axis=-1, keepdims=True) + NORM_EPS)
            o_ref[pl.ds(r0, CHUNK), sv] = (y * og[:, sv]).astype(o_ref.dtype)

    _for_chunks(blk_chunks, chunk, ML_UNROLL)


def _mlstm(zq, zg, gate_bias, conv_w, conv_b, norm_g, batch, l_pad, heads, q_col, pad):
    T = zq.shape[0]
    group = _group(heads, ML_GROUP)
    ngrp = heads // group
    gk, gv = group * HEAD_DK, group * ML_DV
    qk_w = heads * HEAD_DK
    qb = q_col // gk
    kb = qb + ngrp
    vb = (q_col + 2 * qk_w) // gv
    ob = vb + ngrp
    assert q_col % gk == 0 and (q_col + 2 * qk_w) % gv == 0
    n_chunks = l_pad // CHUNK
    blk_chunks = _block_chunks(n_chunks)
    nt = n_chunks // blk_chunks
    tb = blk_chunks * CHUNK
    kern = functools.partial(_mlstm_kernel, blk_chunks=blk_chunks, pad=pad, heads=heads, group=group)
    rows = lambda b, h, t: b * nt + t
    return pl.pallas_call(
        kern,
        grid=(batch, ngrp, nt),
        in_specs=[pl.BlockSpec((tb, LANES), lambda b, h, t: (rows(b, h, t), 0)),
                  pl.BlockSpec((1, LANES), lambda b, h, t: (0, 0)),
                  pl.BlockSpec((tb, gk), lambda b, h, t: (rows(b, h, t), qb + h)),
                  pl.BlockSpec((tb, gk), lambda b, h, t: (rows(b, h, t), kb + h)),
                  pl.BlockSpec((tb, gv), lambda b, h, t: (rows(b, h, t), vb + h)),
                  pl.BlockSpec((tb, gv), lambda b, h, t: (rows(b, h, t), ob + h)),
                  pl.BlockSpec((CONV_W, gk), lambda b, h, t: (0, h)),
                  pl.BlockSpec((CONV_W, gk), lambda b, h, t: (0, ngrp + h)),
                  pl.BlockSpec((1, gk), lambda b, h, t: (0, h)),
                  pl.BlockSpec((1, gk), lambda b, h, t: (0, ngrp + h)),
                  pl.BlockSpec((1, gv), lambda b, h, t: (0, h))],
        out_specs=pl.BlockSpec((tb, gv), lambda b, h, t: (rows(b, h, t), h)),
        out_shape=jax.ShapeDtypeStruct((T, heads * ML_DV), BF16),
        scratch_shapes=[pltpu.VMEM((group, HEAD_DK, ML_DV), F32),
                        pltpu.VMEM((group, 1, HEAD_DK), F32),
                        pltpu.VMEM((group, 1, 1), F32),
                        pltpu.VMEM((CHUNK, gk), F32),
                        pltpu.VMEM((CHUNK, gk), F32)],
        compiler_params=_params(("parallel", "parallel", "arbitrary")),
        name="mlstm",
    )(zg, gate_bias, zq, zq, zq, zq, conv_w.astype(F32), conv_w.astype(F32),
      conv_b.reshape(1, -1).astype(F32), conv_b.reshape(1, -1).astype(F32),
      norm_g.reshape(1, -1).astype(F32))


def kernel(x, meta_tokens, hgrn_lb_logits, norm_ffn1, ffn1_w_gate, ffn1_w_up, ffn1_w_down, norm_mix, w_in, mlstm_conv_w, mlstm_conv_b, mlstm_igate_b, mlstm_fgate_b, hgrn_out_norm, mlstm_out_norm, w_branch_a, w_branch_b, w_out, norm_ffn2, ffn2_w_gate, ffn2_w_up, ffn2_w_down, final_norm):
    batch, seq, d_model = x.shape
    depth = w_in.shape[0]
    n_meta = meta_tokens.shape[0]
    hg_heads = w_branch_a.shape[1] // HG_DV
    ml_heads = mlstm_igate_b.shape[1]
    hg_w = hg_heads * HEAD_DK
    ml_qk = ml_heads * HEAD_DK
    ml_v = ml_heads * ML_DV
    n_gate = 2 * ml_heads
    assert n_gate <= LANES

    pad = (-(n_meta + seq)) % CHUNK
    l_pad = pad + n_meta + seq
    meta = jnp.broadcast_to(meta_tokens[None].astype(x.dtype), (batch, n_meta, d_model))
    h = jnp.concatenate([jnp.zeros((batch, pad, d_model), x.dtype), meta, x], axis=1)
    h = h.reshape(batch * l_pad, d_model)

    gates_col = 4 * hg_w + 2 * ml_qk + 2 * ml_v
    mlstm_q_col = 4 * hg_w
    assert gates_col + LANES <= w_in.shape[2]
    w_in_t = jnp.swapaxes(w_in, 1, 2)
    gate_bias = jnp.pad(jnp.concatenate([mlstm_igate_b, mlstm_fgate_b], axis=1).astype(F32),
                        ((0, 0), (0, LANES - n_gate)))
    w_g1, w_u1, w_d1 = ffn1_w_gate.astype(BF16), ffn1_w_up.astype(BF16), ffn1_w_down.astype(BF16)
    w_g2, w_u2, w_d2 = ffn2_w_gate.astype(BF16), ffn2_w_up.astype(BF16), ffn2_w_down.astype(BF16)
    w_a, w_b, w_o = w_branch_a.astype(BF16), w_branch_b.astype(BF16), w_out.astype(BF16)

    u, ss = _prep(h, norm_ffn1[0])
    for l in range(depth):
        a = _ffn_up(u, ss, w_g1, w_u1, l)
        h, u, ss = _resid_mm(a, w_d1, l, h, 0.5, norm_mix[l])

        zmix = _in_proj(u, ss, w_in_t, l, 0, gates_col, BF16)
        zg = _in_proj(u, ss, w_in_t, l, gates_col, LANES, F32)
        zmrg = _in_proj(u, ss, w_in_t, l, gates_col + n_gate, 2 * d_model, BF16)
        ya = _hgrn(zmix, hgrn_lb_logits, hgrn_out_norm[l], l, batch, l_pad, hg_heads, pad)
        yb = _mlstm(zmix, zg, gate_bias[l:l + 1], mlstm_conv_w[l], mlstm_conv_b[l], mlstm_out_norm[l],
                    batch, l_pad, ml_heads, mlstm_q_col, pad)
        y = _merge(ya, yb, w_a, w_b, l, zmrg)
        h, u, ss = _resid_mm(y, w_o, l, h, 1.0, norm_ffn2[l])

        a = _ffn_up(u, ss, w_g2, w_u2, l)
        next_gain = norm_ffn1[l + 1] if l + 1 < depth else final_norm
        h, u, ss = _resid_mm(a, w_d2, l, h, 0.5, next_gain)

    return _final_norm(h, final_norm, batch, l_pad, seq)
```
